```python
import math
import jax, jax.numpy as jnp
from jax import lax
import numpy as np

D_MODEL = 4096
BATCH = 1
SEQ = 8192
DEPTH = 2

CHUNK = 64
N_META = 16
Q_BLOCK = 128
SCAN_CHUNK = 64
N_EVEN = (DEPTH + 1) // 2
N_ODD = DEPTH // 2
NORM_EPS = 1e-6

A_HEADS = D_MODEL // 256
A_DK = 128
A_DV = 128
A_KWIDTH = A_HEADS * A_DK
A_WIDTH = A_HEADS * A_DV

B_HEADS = D_MODEL // 256
B_NOPE = 128
B_ROPE = 64
B_DV = 128
B_Q_LORA = 1024
B_KV_LORA = 512
B_WIDTH = B_HEADS * B_DV
ROPE_THETA = 10000.0

C_GROUPS = 4
POOL_WINDOWS = (2, 4, 8, 16)
C_WIDTH = D_MODEL // 4
C_GROUP_DIM = C_WIDTH // C_GROUPS

D_INNER = 3 * D_MODEL // 4
D_HEAD_DIM = 64
D_HEADS = D_INNER // D_HEAD_DIM
D_GROUPS = 8
D_STATE = 128
D_CONV = 4
D_XBC = D_INNER + 2 * D_GROUPS * D_STATE

D_FF = 256 * ((8 * D_MODEL // 3 + 255) // 256)
FFN_CONV = 3

AB_IN = 2 * A_KWIDTH + 2 * A_WIDTH + B_Q_LORA + B_KV_LORA + B_ROPE
CD_IN = C_WIDTH + D_INNER + D_XBC + D_HEADS

kernel_name = 'hybrid_hgrn2_mla_pool_ssd_convffn'


def rmsnorm(x, g):
    xf = x.astype(jnp.float32)
    y = xf * lax.rsqrt(jnp.mean(xf * xf, axis=-1, keepdims=True) + NORM_EPS)
    return (y * g.astype(jnp.float32)).astype(x.dtype)


def split_cols(a, sizes):
    idx = [int(v) for v in np.cumsum(sizes)[:-1]]
    return jnp.split(a, idx, axis=-1)


def causal_dwconv(x, w, b):
    k_width, ch = w.shape
    y = lax.conv_general_dilated(x, w[:, None, :].astype(x.dtype), window_strides=(1,),
                                 padding=((k_width - 1, 0),),
                                 dimension_numbers=('NWC', 'WIO', 'NWC'),
                                 feature_group_count=ch)
    return y + b.astype(x.dtype)


def rope_tables(length):
    inv = ROPE_THETA ** (-jnp.arange(0, B_ROPE, 2, dtype=jnp.float32) / B_ROPE)
    ang = jnp.arange(length, dtype=jnp.float32)[:, None] * inv[None, :]
    return jnp.cos(ang), jnp.sin(ang)


def apply_rope(x, cos, sin):
    xf = x.astype(jnp.float32)
    half = B_ROPE // 2
    x1, x2 = xf[..., :half], xf[..., half:]
    c = cos[None, :, None, :]
    s = sin[None, :, None, :]
    return jnp.concatenate([x1 * c - x2 * s, x1 * s + x2 * c], axis=-1).astype(x.dtype)


def hgrn2_chunkwise(q, log_f, k, v):
    bsz, length, heads, dk = q.shape
    dv = v.shape[-1]
    n = length // SCAN_CHUNK

    def to_chunks(t):
        return t.reshape(bsz, n, SCAN_CHUNK, heads, t.shape[-1]).transpose(1, 0, 3, 2, 4)

    tril = jnp.tril(jnp.ones((SCAN_CHUNK, SCAN_CHUNK), dtype=bool))

    def step(state, inp):
        qc, lfc, kc, vc = inp
        b = jnp.cumsum(lfc, axis=2)
        o_inter = jnp.einsum('bhtk,bhkv->bhtv', qc * jnp.exp(b), state)
        rel = b[:, :, :, None, :] - b[:, :, None, :, :]
        dec = jnp.exp(jnp.where(tril[:, :, None], rel, -jnp.inf))
        attn = jnp.einsum('bhik,bhjk,bhijk->bhij', qc, kc, dec)
        o_intra = jnp.einsum('bhij,bhjv->bhiv', attn, vc)
        b_last = b[:, :, -1:, :]
        new_state = (jnp.exp(b_last[:, :, 0, :])[..., None] * state
                     + jnp.einsum('bhtk,bhtv->bhkv', kc * jnp.exp(b_last - b), vc))
        return new_state, o_inter + o_intra

    s0 = jnp.zeros((bsz, heads, dk, dv), jnp.float32)
    _, o = lax.scan(step, s0, (to_chunks(q), to_chunks(log_f), to_chunks(k), to_chunks(v)))
    return o.transpose(1, 0, 3, 2, 4).reshape(bsz, length, heads, dv)


def block_causal_attention(q, k, v, chunk_id):
    bsz, length, heads, dqk = q.shape
    nb = length // Q_BLOCK
    scale = dqk ** -0.5
    qb = q.reshape(bsz, nb, Q_BLOCK, heads, dqk).transpose(1, 0, 2, 3, 4)
    cb = chunk_id.reshape(nb, Q_BLOCK)

    def one_block(args):
        qi, ci = args
        s = jnp.einsum('bqhd,bkhd->bhqk', qi, k, preferred_element_type=jnp.float32) * scale
        mask = chunk_id[None, :] <= ci[:, None]
        p = jax.nn.softmax(jnp.where(mask[None, None], s, -jnp.inf), axis=-1)
        return jnp.einsum('bhqk,bkhd->bqhd', p.astype(v.dtype), v)

    o = lax.map(one_block, (qb, cb))
    return o.transpose(1, 0, 2, 3, 4).reshape(bsz, length, heads * v.shape[-1])


def multiscale_pool(u, pool_w, pool_scale):
    bsz, length, _ = u.shape
    groups = u.astype(jnp.float32).reshape(bsz, length, C_GROUPS, C_GROUP_DIM)
    cs = jnp.cumsum(groups, axis=1)
    t = jnp.arange(length)
    outs = []
    for gi, w in enumerate(POOL_WINDOWS):
        c = cs[:, :, gi]
        lagged = jnp.pad(c, ((0, 0), (w, 0), (0, 0)))[:, :length]
        cnt = jnp.minimum(t + 1, w).astype(jnp.float32)[None, :, None]
        outs.append((c - lagged) / cnt - groups[:, :, gi])
    pooled = jnp.stack(outs, axis=2)
    y = jnp.einsum('blgc,gcd->blgd', pooled, pool_w.astype(jnp.float32)).reshape(bsz, length, C_WIDTH)
    return (y * pool_scale.astype(jnp.float32)).astype(u.dtype)


def ssd_chunked(xh, dt, a, bm, cm):
    bsz, length, heads, hp = xh.shape
    hg = heads // D_GROUPS
    n = length // SCAN_CHUNK
    T = SCAN_CHUNK
    xd = (xh * dt[..., None]).reshape(bsz, n, T, D_GROUPS, hg, hp)
    la = (dt * a).reshape(bsz, n, T, D_GROUPS, hg).transpose(0, 3, 4, 1, 2)
    a_cs = jnp.cumsum(la, axis=-1)
    bc = bm.reshape(bsz, n, T, D_GROUPS, D_STATE)
    cc = cm.reshape(bsz, n, T, D_GROUPS, D_STATE)
    tril = jnp.tril(jnp.ones((T, T), dtype=bool))
    seg = a_cs[..., :, None] - a_cs[..., None, :]
    lmat = jnp.exp(jnp.where(tril, seg, -jnp.inf))
    cb = jnp.einsum('bclgn,bcsgn->bcgls', cc, bc)
    y_diag = jnp.einsum('bcgls,bghcls,bcsghp->bclghp', cb, lmat, xd)
    decay_states = jnp.exp(a_cs[..., -1:] - a_cs)
    states = jnp.einsum('bcsgn,bghcs,bcsghp->bcghpn', bc, decay_states, xd)
    chunk_decay = jnp.exp(a_cs[..., -1])

    def step(hstate, inp):
        st, dec = inp
        return dec[..., None, None] * hstate + st, hstate

    h0 = jnp.zeros((bsz, D_GROUPS, hg, hp, D_STATE), jnp.float32)
    _, prev = lax.scan(step, h0, (states.transpose(1, 0, 2, 3, 4, 5), chunk_decay.transpose(3, 0, 1, 2)))
    prev = prev.transpose(1, 0, 2, 3, 4, 5)
    y_off = jnp.einsum('bclgn,bcghpn,bghcl->bclghp', cc, prev, jnp.exp(a_cs))
    return (y_diag + y_off).reshape(bsz, length, heads, hp)


def mixer_hgrn2_mla(xn, lb, w_in, hgrn_g, q_norm_g, kv_norm_g, w_uq, w_ukv, w_out, cos, sin, chunk_id):
    bsz, length, _ = xn.shape
    proj = xn @ w_in
    q_a, f_a, i_a, g_a, c_q, c_kv, k_r = split_cols(
        proj, [A_KWIDTH, A_KWIDTH, A_WIDTH, A_WIDTH, B_Q_LORA, B_KV_LORA, B_ROPE])
    lbh = lb.reshape(A_HEADS, A_DK)
    fp = f_a.astype(jnp.float32).reshape(bsz, length, A_HEADS, A_DK)
    log_f = jnp.log(lbh + (1.0 - lbh) * jax.nn.sigmoid(fp))
    k_a = (1.0 - lbh) * jax.nn.sigmoid(-fp)
    o_a = hgrn2_chunkwise(q_a.astype(jnp.float32).reshape(bsz, length, A_HEADS, A_DK), log_f, k_a,
                          i_a.astype(jnp.float32).reshape(bsz, length, A_HEADS, A_DV))
    o_a = rmsnorm(o_a, hgrn_g).reshape(bsz, length, A_WIDTH)
    o_a = (o_a * jax.nn.silu(g_a.astype(jnp.float32))).astype(xn.dtype)
    qf = (rmsnorm(c_q, q_norm_g) @ w_uq).reshape(bsz, length, B_HEADS, B_NOPE + B_ROPE)
    kvf = (rmsnorm(c_kv, kv_norm_g) @ w_ukv).reshape(bsz, length, B_HEADS, B_NOPE + B_DV)
    q_pe = apply_rope(qf[..., B_NOPE:], cos, sin)
    k_pe = apply_rope(k_r[:, :, None, :], cos, sin)
    q = jnp.concatenate([qf[..., :B_NOPE], q_pe], axis=-1)
    k = jnp.concatenate([kvf[..., :B_NOPE], jnp.broadcast_to(k_pe, (bsz, length, B_HEADS, B_ROPE))], axis=-1)
    o_b = block_causal_attention(q, k, kvf[..., B_NOPE:], chunk_id)
    return jnp.concatenate([o_a, o_b.astype(xn.dtype)], axis=-1) @ w_out


def mixer_pool_ssd(xn, w_in, pool_w, pool_scale, conv_w, conv_b, dt_bias, a_log, d_skip, norm_g, w_out):
    bsz, length, _ = xn.shape
    proj = xn @ w_in
    u_c, z, xbc, dt_raw = split_cols(proj, [C_WIDTH, D_INNER, D_XBC, D_HEADS])
    o_c = multiscale_pool(u_c, pool_w, pool_scale)
    xbc = jax.nn.silu(causal_dwconv(xbc, conv_w, conv_b))
    xs, bm, cm = split_cols(xbc, [D_INNER, D_GROUPS * D_STATE, D_GROUPS * D_STATE])
    dt = jax.nn.softplus(dt_raw.astype(jnp.float32) + dt_bias.astype(jnp.float32))
    a = -jnp.exp(a_log.astype(jnp.float32))
    xh = xs.astype(jnp.float32).reshape(bsz, length, D_HEADS, D_HEAD_DIM)
    y = ssd_chunked(xh, dt, a,
                    bm.astype(jnp.float32).reshape(bsz, length, D_GROUPS, D_STATE),
                    cm.astype(jnp.float32).reshape(bsz, length, D_GROUPS, D_STATE))
    y = (y + xh * d_skip.astype(jnp.float32)[:, None]).reshape(bsz, length, D_INNER)
    y = rmsnorm(y * jax.nn.silu(z.astype(jnp.float32)), norm_g).astype(xn.dtype)
    return jnp.concatenate([o_c, y], axis=-1) @ w_out


def conv_ffn(xn, w_up, conv_w, conv_b, w_down):
    u = causal_dwconv(xn @ w_up, conv_w, conv_b)
    gate, val = jnp.split(u, 2, axis=-1)
    return (jax.nn.silu(gate) * val) @ w_down


def setup_inputs(seed: int = 0) -> dict:
    key = jax.random.key(seed)
    ks = jax.random.split(key, 25)
    f32 = jnp.float32

    def nrm(k, shape, scale):
        return jax.random.normal(k, shape, f32) * scale

    def gain(k, shape):
        return 1.0 + 0.05 * jax.random.normal(k, shape, f32)

    dt0 = jnp.exp(jax.random.uniform(ks[16], (N_ODD, D_HEADS), f32)
                  * (math.log(0.1) - math.log(0.001)) + math.log(0.001))
    return {
        'x': nrm(ks[0], (BATCH, SEQ, D_MODEL), 1.0),
        'meta_tokens': nrm(ks[1], (N_META, D_MODEL), 1.0),
        'lb_logits': nrm(ks[2], (DEPTH + 1, A_KWIDTH), 0.1),
        'norm_g': gain(ks[3], (DEPTH, 4, D_MODEL)),
        'ab_w_in': nrm(ks[4], (N_EVEN, D_MODEL, AB_IN), D_MODEL ** -0.5),
        'hgrn_norm_g': gain(ks[5], (N_EVEN, A_HEADS, A_DV)),
        'mla_q_norm_g': gain(ks[6], (N_EVEN, B_Q_LORA)),
        'mla_kv_norm_g': gain(ks[7], (N_EVEN, B_KV_LORA)),
        'mla_w_uq': nrm(ks[8], (N_EVEN, B_Q_LORA, B_HEADS * (B_NOPE + B_ROPE)), B_Q_LORA ** -0.5),
        'mla_w_ukv': nrm(ks[9], (N_EVEN, B_KV_LORA, B_HEADS * (B_NOPE + B_DV)), B_KV_LORA ** -0.5),
        'ab_w_out': nrm(ks[10], (N_EVEN, A_WIDTH + B_WIDTH, D_MODEL), (A_WIDTH + B_WIDTH) ** -0.5),
        'cd_w_in': nrm(ks[11], (N_ODD, D_MODEL, CD_IN), D_MODEL ** -0.5),
        'pool_w': nrm(ks[12], (N_ODD, C_GROUPS, C_GROUP_DIM, C_GROUP_DIM), C_GROUP_DIM ** -0.5),
        'pool_scale': gain(ks[13], (N_ODD, C_WIDTH)),
        'ssm_conv_w': nrm(ks[14], (N_ODD, D_CONV, D_XBC), D_CONV ** -0.5),
        'ssm_conv_b': nrm(ks[15], (N_ODD, D_XBC), 0.02),
        'ssm_dt_bias': dt0 + jnp.log(-jnp.expm1(-dt0)),
        'ssm_a_log': jnp.log(jax.random.uniform(ks[17], (N_ODD, D_HEADS), f32, 1.0, 16.0)),
        'ssm_d': gain(ks[18], (N_ODD, D_HEADS)),
        'ssm_norm_g': gain(ks[19], (N_ODD, D_INNER)),
        'cd_w_out': nrm(ks[20], (N_ODD, C_WIDTH + D_INNER, D_MODEL), (C_WIDTH + D_INNER) ** -0.5),
        'ffn_w_up': nrm(ks[21], (DEPTH, D_MODEL, 2 * D_FF), D_MODEL ** -0.5),
        'ffn_conv_w': nrm(ks[22], (DEPTH, FFN_CONV, 2 * D_FF), FFN_CONV ** -0.5),
        'ffn_conv_b': nrm(ks[23], (DEPTH, 2 * D_FF), 0.02),
        'ffn_w_down': nrm(ks[24], (DEPTH, D_FF, D_MODEL), D_FF ** -0.5),
    }


def reference(x, meta_tokens, lb_logits, norm_g, ab_w_in, hgrn_norm_g, mla_q_norm_g, mla_kv_norm_g,
              mla_w_uq, mla_w_ukv, ab_w_out, cd_w_in, pool_w, pool_scale, ssm_conv_w, ssm_conv_b,
              ssm_dt_bias, ssm_a_log, ssm_d, ssm_norm_g, cd_w_out, ffn_w_up, ffn_conv_w, ffn_conv_b,
              ffn_w_down):
    bsz, seq, _ = x.shape
    total = N_META + seq
    lp = -(-total // Q_BLOCK) * Q_BLOCK
    meta = jnp.broadcast_to(meta_tokens.astype(x.dtype)[None], (bsz, N_META, D_MODEL))
    h = jnp.concatenate([meta, x], axis=1)
    h = jnp.pad(h, ((0, 0), (0, lp - total), (0, 0)))
    pos = jnp.arange(lp, dtype=jnp.int32)
    chunk_id = jnp.where(pos < N_META, 0, 1 + (pos - N_META) // CHUNK)
    cos, sin = rope_tables(lp)
    lb_all = jnp.cumsum(jax.nn.softmax(lb_logits.astype(jnp.float32), axis=0), axis=0)
    for layer in range(DEPTH):
        g = norm_g[layer]
        j = layer // 2
        xn = rmsnorm(h, g[0])
        if layer % 2 == 0:
            mixed = mixer_hgrn2_mla(xn, lb_all[layer], ab_w_in[j], hgrn_norm_g[j], mla_q_norm_g[j],
                                    mla_kv_norm_g[j], mla_w_uq[j], mla_w_ukv[j], ab_w_out[j],
                                    cos, sin, chunk_id)
        else:
            mixed = mixer_pool_ssd(xn, cd_w_in[j], pool_w[j], pool_scale[j], ssm_conv_w[j], ssm_conv_b[j],
                                   ssm_dt_bias[j], ssm_a_log[j], ssm_d[j], ssm_norm_g[j], cd_w_out[j])
        h = h + rmsnorm(mixed, g[1])
        ff = conv_ffn(rmsnorm(h, g[2]), ffn_w_up[layer], ffn_conv_w[layer], ffn_conv_b[layer], ffn_w_down[layer])
        h = h + rmsnorm(ff, g[3])
    return h[:, N_META:total]
```

```python
import functools
import math

import numpy as np
import jax
import jax.numpy as jnp
from jax import lax
from jax.experimental import pallas as pl
from jax.experimental.pallas import tpu as pltpu

F32 = jnp.float32
BF16 = jnp.bfloat16
HIGHEST = lax.Precision.HIGHEST

CHUNK = 64
N_META = 16
Q_BLOCK = 128
NORM_EPS = 1e-6
A_DK = 128
A_DV = 128
B_NOPE = 128
B_ROPE = 64
B_DV = 128
B_Q_LORA = 1024
B_KV_LORA = 512
ROPE_THETA = 10000.0
C_GROUPS = 4
POOL_WINDOWS = (2, 4, 8, 16)
D_HEAD_DIM = 64
D_GROUPS = 8
D_STATE = 128
D_CONV = 4
FFN_CONV = 3

SCAN_T = 64
SUB_T = 16
LANES = 128
HALO_BF16 = 16
HALO_F32 = 8
MIB = 1024 * 1024


def _cp(sem, vmem_mib=48):
    return pltpu.CompilerParams(dimension_semantics=sem, vmem_limit_bytes=vmem_mib * MIB)


def _pick(n, cands):
    for c in cands:
        if n % c == 0:
            return c
    raise ValueError(f"no tile for {n} in {cands}")


def _rms(x, g):
    return x * lax.rsqrt(jnp.mean(x * x, axis=-1, keepdims=True) + NORM_EPS) * g


def _mm_body(x_ref, w_ref, o_ref):
    o_ref[...] = jnp.dot(x_ref[...], w_ref[...], preferred_element_type=F32).astype(o_ref.dtype)


def _matmul(x, w, *, k, xblk=0, col0=0, ncols, tm, tn, out_dtype, name, vmem_mib=48):
    rows = x.shape[0]
    assert w.shape[0] == k and col0 % tn == 0 and ncols % tn == 0 and rows % tm == 0
    cb = col0 // tn
    return pl.pallas_call(
        _mm_body,
        grid=(rows // tm, ncols // tn),
        in_specs=[pl.BlockSpec((tm, k), lambda i, j: (i, xblk)),
                  pl.BlockSpec((k, tn), lambda i, j: (0, cb + j))],
        out_specs=pl.BlockSpec((tm, tn), lambda i, j: (i, j)),
        out_shape=jax.ShapeDtypeStruct((rows, ncols), out_dtype),
        compiler_params=_cp(("parallel", "parallel"), vmem_mib),
        name=name,
    )(x, w)


def _rownorm_body(x_ref, g_ref, o_ref):
    o_ref[...] = _rms(x_ref[...].astype(F32), g_ref[...]).astype(o_ref.dtype)


def _rownorm(x, g, *, width, xblk=0, tm, name):
    rows = x.shape[0]
    return pl.pallas_call(
        _rownorm_body,
        grid=(rows // tm,),
        in_specs=[pl.BlockSpec((tm, width), lambda i: (i, xblk)),
                  pl.BlockSpec((1, width), lambda i: (0, 0))],
        out_specs=pl.BlockSpec((tm, width), lambda i: (i, 0)),
        out_shape=jax.ShapeDtypeStruct((rows, width), BF16),
        compiler_params=_cp(("parallel",)),
        name=name,
    )(x, g.reshape(1, width).astype(F32))


def _addnorm_body(h_ref, m_ref, g1_ref, g2_ref, ho_ref, xo_ref):
    hn = h_ref[...] + _rms(m_ref[...], g1_ref[...])
    ho_ref[...] = hn
    xo_ref[...] = _rms(hn, g2_ref[...]).astype(xo_ref.dtype)


def _add_body(h_ref, m_ref, g1_ref, ho_ref):
    ho_ref[...] = h_ref[...] + _rms(m_ref[...], g1_ref[...])


def _addnorm(h, m, g_post, g_next, *, tm, name):
    rows, d = h.shape
    row_spec = pl.BlockSpec((tm, d), lambda i: (i, 0))
    g_spec = pl.BlockSpec((1, d), lambda i: (0, 0))
    if g_next is None:
        return pl.pallas_call(
            _add_body, grid=(rows // tm,),
            in_specs=[row_spec, row_spec, g_spec], out_specs=row_spec,
            out_shape=jax.ShapeDtypeStruct((rows, d), F32),
            compiler_params=_cp(("parallel",)), name=name,
        )(h, m, g_post.reshape(1, d)), None
    return pl.pallas_call(
        _addnorm_body, grid=(rows // tm,),
        in_specs=[row_spec, row_spec, g_spec, g_spec], out_specs=[row_spec, row_spec],
        out_shape=[jax.ShapeDtypeStruct((rows, d), F32), jax.ShapeDtypeStruct((rows, d), BF16)],
        compiler_params=_cp(("parallel",)), name=name,
    )(h, m, g_post.reshape(1, d), g_next.reshape(1, d))


def _ffn_up_body(x_ref, halo_ref, wg_ref, wv_ref, cwg_ref, cwv_ref, cbg_ref, cbv_ref, o_ref, xcat_ref):
    i = pl.program_id(0)

    @pl.when(pl.program_id(1) == 0)
    def _():
        hl = halo_ref[...]
        xcat_ref[0:HALO_BF16, :] = jnp.where(i == 0, jnp.zeros_like(hl), hl)
        xcat_ref[HALO_BF16:, :] = x_ref[...]

    xc = xcat_ref[...]

    def branch(w_ref, cw_ref, cb_ref):
        u = jnp.dot(xc, w_ref[...], preferred_element_type=F32)
        cw = cw_ref[...]
        y = (cw[2:3, :] * u + cw[1:2, :] * pltpu.roll(u, 1, 0)
             + cw[0:1, :] * pltpu.roll(u, 2, 0) + cb_ref[...])
        return y[HALO_BF16:, :]

    gate = branch(wg_ref, cwg_ref, cbg_ref)
    val = branch(wv_ref, cwv_ref, cbv_ref)
    o_ref[...] = (jax.nn.silu(gate) * val).astype(o_ref.dtype)


def _ffn_up(xn, w_up, conv_w, conv_b, *, tm, tn, name):
    rows, d = xn.shape
    d_ff = w_up.shape[1] // 2
    nj = d_ff // tn
    hb = tm // HALO_BF16
    cb = conv_b.reshape(1, 2 * d_ff)
    return pl.pallas_call(
        _ffn_up_body,
        grid=(rows // tm, nj),
        in_specs=[pl.BlockSpec((tm, d), lambda i, j: (i, 0)),
                  pl.BlockSpec((HALO_BF16, d), lambda i, j: (jnp.maximum(i * hb - 1, 0), 0)),
                  pl.BlockSpec((d, tn), lambda i, j: (0, j)),
                  pl.BlockSpec((d, tn), lambda i, j: (0, nj + j)),
                  pl.BlockSpec((FFN_CONV, tn), lambda i, j: (0, j)),
                  pl.BlockSpec((FFN_CONV, tn), lambda i, j: (0, nj + j)),
                  pl.BlockSpec((1, tn), lambda i, j: (0, j)),
                  pl.BlockSpec((1, tn), lambda i, j: (0, nj + j))],
        out_specs=pl.BlockSpec((tm, tn), lambda i, j: (i, j)),
        out_shape=jax.ShapeDtypeStruct((rows, d_ff), BF16),
        scratch_shapes=[pltpu.VMEM((tm + HALO_BF16, d), BF16)],
        compiler_params=_cp(("parallel", "arbitrary")),
        name=name,
    )(xn, xn, w_up, w_up, conv_w, conv_w, cb, cb)


def _hgrn2_body(q_ref, f_ref, v_ref, g_ref, lb_ref, hg_ref, o_ref, st_ref, *, nsub):
    @pl.when(pl.program_id(1) == 0)
    def _():
        st_ref[...] = jnp.zeros_like(st_ref)

    t = SCAN_T
    lb = lb_ref[...]
    hg = hg_ref[...]
    ri = lax.broadcasted_iota(jnp.int32, (t, t), 0)
    ci = lax.broadcasted_iota(jnp.int32, (t, t), 1)
    tril = ci <= ri
    trilf = tril.astype(F32)
    jrow = lax.broadcasted_iota(jnp.int32, (t, A_DK), 0)
    nt = (((1,), (1,)), ((), ()))
    tn = (((0,), (0,)), ((), ()))

    for s in range(nsub):
        sl = slice(s * t, (s + 1) * t)
        q = q_ref[sl, :]
        fp = f_ref[sl, :]
        v = v_ref[sl, :].astype(BF16)
        lf = jnp.log(lb + (1.0 - lb) * jax.nn.sigmoid(fp))
        k = (1.0 - lb) * jax.nn.sigmoid(-fp)
        b = jnp.dot(trilf, lf, precision=HIGHEST, preferred_element_type=F32)
        st = st_ref[...]
        o = lax.dot_general((q * jnp.exp(b)).astype(BF16), st.astype(BF16), nt,
                            preferred_element_type=F32)
        rows = []
        for blk in range(t // SUB_T):
            lo = blk * SUB_T
            r = b[lo + SUB_T // 2 - 1:lo + SUB_T // 2, :]
            qt = (q[lo:lo + SUB_T, :] * jnp.exp(b[lo:lo + SUB_T, :] - r)).astype(BF16)
            kh = jnp.where(jrow < lo + SUB_T, k * jnp.exp(r - b), 0.0).astype(BF16)
            rows.append(lax.dot_general(qt, kh, nt, preferred_element_type=F32))
        a = jnp.where(tril, jnp.concatenate(rows, axis=0), 0.0)
        o = o + jnp.dot(a.astype(BF16), v, preferred_element_type=F32)
        bl = b[t - 1:t, :]
        kd = (k * jnp.exp(bl - b)).astype(BF16)
        st_ref[...] = st * jnp.exp(bl) + lax.dot_general(v, kd, tn, preferred_element_type=F32)
        o = _rms(o, hg)
        o_ref[sl, :] = (o * jax.nn.silu(g_ref[sl, :])).astype(o_ref.dtype)


def _hgrn2(proj, lb, hg, *, heads, tb, name):
    rows = proj.shape[0]
    blk = lambda off: pl.BlockSpec((tb, A_DK), lambda h, c, off=off: (c, off + h))
    vec = pl.BlockSpec((pl.Squeezed(), 1, A_DK), lambda h, c: (h, 0, 0))
    return pl.pallas_call(
        functools.partial(_hgrn2_body, nsub=tb // SCAN_T),
        grid=(heads, rows // tb),
        in_specs=[blk(0), blk(heads), blk(2 * heads), blk(3 * heads), vec, vec],
        out_specs=pl.BlockSpec((tb, A_DV), lambda h, c: (c, h)),
        out_shape=jax.ShapeDtypeStruct((rows, heads * A_DV), BF16),
        scratch_shapes=[pltpu.VMEM((A_DV, A_DK), F32)],
        compiler_params=_cp(("parallel", "arbitrary")),
        name=name,
    )(proj, proj, proj, proj, lb.reshape(heads, 1, A_DK), hg.reshape(heads, 1, A_DV))


def _kprep_body(kr_ref, cs_ref, o_ref):
    t = kr_ref[...] * cs_ref[...]
    o_ref[...] = (t + pltpu.roll(t, B_ROPE, 1)).astype(o_ref.dtype)


def _kprep(krp, cs, *, tm, name):
    rows = krp.shape[0]
    spec = pl.BlockSpec((tm, LANES), lambda i: (i, 0))
    return pl.pallas_call(
        _kprep_body, grid=(rows // tm,), in_specs=[spec, spec], out_specs=spec,
        out_shape=jax.ShapeDtypeStruct((rows, LANES), BF16),
        compiler_params=_cp(("parallel",)), name=name,
    )(krp, cs)


def _attn_body(qn_ref, qp_ref, cs_ref, kn_ref, kp_ref, v_ref, o_ref, *, tq, tk, n_kv, scale):
    i = pl.program_id(1)
    qs = i * tq
    qn = (qn_ref[...] * scale).astype(BF16)
    qp = (qp_ref[...] * cs_ref[...] * scale).astype(BF16)
    off = CHUNK - N_META
    last_cid = (qs + tq - 1 + off) // CHUNK
    n_vis = jnp.minimum((CHUNK * last_cid + N_META - 1) // tk + 1, n_kv)
    q_cid = (qs + lax.broadcasted_iota(jnp.int32, (tq, tk), 0) + off) // CHUNK
    k_iota = lax.broadcasted_iota(jnp.int32, (tq, tk), 1) + off
    nt = (((1,), (1,)), ((), ()))

    def step(j, carry):
        m, l, acc = carry
        ks = pl.multiple_of(j * tk, tk)
        kn = kn_ref[pl.ds(ks, tk), :]
        kp = kp_ref[pl.ds(ks, tk), :]
        v = v_ref[pl.ds(ks, tk), :]
        s = (lax.dot_general(qn, kn, nt, preferred_element_type=F32)
             + lax.dot_general(qp, kp, nt, preferred_element_type=F32))
        s = jnp.where((k_iota + ks) // CHUNK <= q_cid, s, -jnp.inf)
        m_new = jnp.maximum(m, jnp.max(s, axis=-1, keepdims=True))
        p = jnp.exp(s - m_new)
        alpha = jnp.exp(m - m_new)
        l = alpha * l + jnp.sum(p, axis=-1, keepdims=True)
        acc = alpha * acc + jnp.dot(p.astype(BF16), v, preferred_element_type=F32)
        return m_new, l, acc

    init = (jnp.full((tq, 1), -jnp.inf, F32), jnp.zeros((tq, 1), F32), jnp.zeros((tq, B_DV), F32))
    _, l, acc = lax.fori_loop(0, n_vis, step, init)
    o_ref[...] = (acc / l).astype(o_ref.dtype)


def _attention(qf, cs, kvf, kp, *, heads, tq, tk, name):
    rows = qf.shape[0]
    scale = float(B_NOPE + B_ROPE) ** -0.5
    full = lambda f: pl.BlockSpec((rows, LANES), f)
    return pl.pallas_call(
        functools.partial(_attn_body, tq=tq, tk=tk, n_kv=rows // tk, scale=scale),
        grid=(heads, rows // tq),
        in_specs=[pl.BlockSpec((tq, LANES), lambda h, i: (i, h)),
                  pl.BlockSpec((tq, LANES), lambda h, i: (i, heads + h)),
                  pl.BlockSpec((tq, LANES), lambda h, i: (i, 0)),
                  full(lambda h, i: (0, 2 * h)),
                  full(lambda h, i: (0, 0)),
                  full(lambda h, i: (0, 2 * h + 1))],
        out_specs=pl.BlockSpec((tq, B_DV), lambda h, i: (i, h)),
        out_shape=jax.ShapeDtypeStruct((rows, heads * B_DV), BF16),
        compiler_params=_cp(("parallel", "parallel")),
        name=name,
    )(qf, qf, cs, kvf, kp, kvf)


def _pool_body(u_ref, halo_ref, w_ref, sc_ref, o_ref, *, tm, gdim):
    i = pl.program_id(0)
    hl = halo_ref[...]
    hl = jnp.where(i == 0, jnp.zeros_like(hl), hl)
    halo = hl.shape[0]
    pos = i * tm + lax.broadcasted_iota(jnp.int32, (tm, gdim), 0)
    for gi, win in enumerate(POOL_WINDOWS):
        cols = slice(gi * gdim, (gi + 1) * gdim)
        x = u_ref[:, cols]
        s = jnp.concatenate([hl[:, cols], x], axis=0)
        sh = 1
        while sh < win:
            s = s + pltpu.roll(s, sh, 0)
            sh *= 2
        cnt = jnp.minimum(pos + 1, win).astype(F32)
        pooled = s[halo:, :] / cnt - x
        y = jnp.dot(pooled.astype(BF16), w_ref[gi], preferred_element_type=F32)
        o_ref[:, cols] = (y * sc_ref[:, cols]).astype(o_ref.dtype)


def _pool(zu, pool_w, pool_scale, *, ublk, tm, name):
    rows = zu.shape[0]
    groups, gdim, _ = pool_w.shape
    cw = groups * gdim
    halo = 2 * HALO_F32
    hb = tm // halo
    return pl.pallas_call(
        functools.partial(_pool_body, tm=tm, gdim=gdim),
        grid=(rows // tm,),
        in_specs=[pl.BlockSpec((tm, cw), lambda i: (i, ublk)),
                  pl.BlockSpec((halo, cw), lambda i: (jnp.maximum(i * hb - 1, 0), ublk)),
                  pl.BlockSpec((groups, gdim, gdim), lambda i: (0, 0, 0)),
                  pl.BlockSpec((1, cw), lambda i: (0, 0))],
        out_specs=pl.BlockSpec((tm, cw), lambda i: (i, 0)),
        out_shape=jax.ShapeDtypeStruct((rows, cw), BF16),
        compiler_params=_cp(("parallel",)),
        name=name,
    )(zu, zu, pool_w.astype(BF16), pool_scale.reshape(1, cw))


def _ssd_conv_body(x_ref, halo_ref, cw_ref, cb_ref, o_ref):
    i = pl.program_id(0)
    hl = halo_ref[...]
    hl = jnp.where(i == 0, jnp.zeros_like(hl), hl)
    halo = hl.shape[0]
    x = jnp.concatenate([hl, x_ref[...]], axis=0)
    cw = cw_ref[...]
    y = cw[D_CONV - 1:D_CONV, :] * x + cb_ref[...]
    for kk in range(1, D_CONV):
        y = y + cw[D_CONV - 1 - kk:D_CONV - kk, :] * pltpu.roll(x, kk, 0)
    o_ref[...] = jax.nn.silu(y[halo:, :])


def _ssd_conv(xbc, conv_w, conv_b, *, tm, tc, name):
    rows, ch = xbc.shape
    hb = tm // HALO_F32
    return pl.pallas_call(
        _ssd_conv_body,
        grid=(rows // tm, ch // tc),
        in_specs=[pl.BlockSpec((tm, tc), lambda i, j: (i, j)),
                  pl.BlockSpec((HALO_F32, tc), lambda i, j: (jnp.maximum(i * hb - 1, 0), j)),
                  pl.BlockSpec((D_CONV, tc), lambda i, j: (0, j)),
                  pl.BlockSpec((1, tc), lambda i, j: (0, j))],
        out_specs=pl.BlockSpec((tm, tc), lambda i, j: (i, j)),
        out_shape=jax.ShapeDtypeStruct((rows, ch), F32),
        compiler_params=_cp(("parallel", "parallel")),
        name=name,
    )(xbc, xbc, conv_w, conv_b.reshape(1, ch))


def _ssd_body(xs_ref, b_ref, c_ref, dt_ref, z_ref, dtb_ref, a_ref, dsk_ref, o_ref, st_ref, *, nsub, hpg):
    g = pl.program_id(0)

    @pl.when(pl.program_id(1) == 0)
    def _():
        st_ref[...] = jnp.zeros_like(st_ref)

    t = SCAN_T
    gw = hpg * D_HEAD_DIM
    a_row = a_ref[...]
    dsk = dsk_ref[...]
    dtb = dtb_ref[...]
    hrow = lax.broadcasted_iota(jnp.int32, (LANES, gw), 0)
    hcol = lax.broadcasted_iota(jnp.int32, (LANES, gw), 1) // D_HEAD_DIM + g * hpg
    sel = (hrow == hcol).astype(F32)
    ri = lax.broadcasted_iota(jnp.int32, (t, t), 0)
    ci = lax.broadcasted_iota(jnp.int32, (t, t), 1)
    trilf = (ci <= ri).astype(F32)
    pr = lax.broadcasted_iota(jnp.int32, (t, 2 * t), 0)
    pc = lax.broadcasted_iota(jnp.int32, (t, 2 * t), 1)
    pj = pc % t
    tril2 = pj <= pr
    triu2f = (pr <= pj).astype(F32)
    left = pc < t
    nt = (((1,), (1,)), ((), ()))
    tn = (((0,), (0,)), ((), ()))

    for s in range(nsub):
        sl = slice(s * t, (s + 1) * t)
        xs = xs_ref[sl, :]
        bb = b_ref[sl, :].astype(BF16)
        cc = c_ref[sl, :].astype(BF16)
        dt = jax.nn.softplus(dt_ref[sl, :] + dtb)
        dtx = jnp.dot(dt, sel, precision=HIGHEST, preferred_element_type=F32)
        la = dtx * a_row
        acs = jnp.dot(trilf, la, precision=HIGHEST, preferred_element_type=F32)
        xd = xs * dtx
        last = acs[t - 1:t, :]
        st = st_ref[...]
        y = jnp.exp(acs) * jnp.dot(cc, st.astype(BF16), preferred_element_type=F32)
        st_ref[...] = st * jnp.exp(last) + lax.dot_general(
            bb, (xd * jnp.exp(last - acs)).astype(BF16), tn, preferred_element_type=F32)
        cb2 = lax.dot_general(cc, jnp.concatenate([bb, bb], axis=0), nt, preferred_element_type=F32)
        diag = []
        for p in range(hpg // 2):
            cols = slice(p * 2 * D_HEAD_DIM, (p + 1) * 2 * D_HEAD_DIM)
            row_cs = jnp.sum(la[:, cols] * triu2f, axis=0, keepdims=True)
            lmat = jnp.where(tril2, jnp.exp(jnp.minimum(acs[:, cols] - row_cs, 0.0)), 0.0)
            xdp = xd[:, cols]
            rhs = jnp.concatenate([jnp.where(left, xdp, 0.0), jnp.where(left, 0.0, xdp)], axis=0)
            diag.append(jnp.dot((cb2 * lmat).astype(BF16), rhs.astype(BF16), preferred_element_type=F32))
        y = y + jnp.concatenate(diag, axis=1) + xs * dsk
        o_ref[sl, :] = y * jax.nn.silu(z_ref[sl, :])


def _ssd(xbc, dt, zu, dt_bias, a_ch, d_ch, *, heads, tb, name):
    rows = xbc.shape[0]
    inner = heads * D_HEAD_DIM
    hpg = heads // D_GROUPS
    gw = hpg * D_HEAD_DIM
    assert hpg % 2 == 0 and 2 * D_HEAD_DIM == LANES and SCAN_T == D_HEAD_DIM
    nxb = inner // LANES
    return pl.pallas_call(
        functools.partial(_ssd_body, nsub=tb // SCAN_T, hpg=hpg),
        grid=(D_GROUPS, rows // tb),
        in_specs=[pl.BlockSpec((tb, gw), lambda g, c: (c, g)),
                  pl.BlockSpec((tb, D_STATE), lambda g, c: (c, nxb + g)),
                  pl.BlockSpec((tb, D_STATE), lambda g, c: (c, nxb + D_GROUPS + g)),
                  pl.BlockSpec((tb, LANES), lambda g, c: (c, 0)),
                  pl.BlockSpec((tb, gw), lambda g, c: (c, g)),
                  pl.BlockSpec((1, LANES), lambda g, c: (0, 0)),
                  pl.BlockSpec((1, gw), lambda g, c: (0, g)),
                  pl.BlockSpec((1, gw), lambda g, c: (0, g))],
        out_specs=pl.BlockSpec((tb, gw), lambda g, c: (c, g)),
        out_shape=jax.ShapeDtypeStruct((rows, inner), F32),
        scratch_shapes=[pltpu.VMEM((D_STATE, gw), F32)],
        compiler_params=_cp(("parallel", "arbitrary")),
        name=name,
    )(xbc, xbc, xbc, dt, zu, dt_bias, a_ch, d_ch)


def _rot_half_cols():
    half = B_ROPE // 2
    idx = np.concatenate([np.arange(half, B_ROPE), np.arange(0, half)])
    sign = np.concatenate([-np.ones(half), np.ones(half)])
    return idx, sign


def _layout_w_in_ab(w, a_kw, a_w):
    kr0 = 2 * a_kw + 2 * a_w + B_Q_LORA + B_KV_LORA
    idx, sign = _rot_half_cols()
    cols = np.concatenate([np.arange(w.shape[1]), kr0 + idx])
    sg = np.concatenate([np.ones(w.shape[1]), sign])
    return w[:, cols] * jnp.asarray(sg, w.dtype)[None, :]


def _layout_w_uq(w, heads):
    dq = B_NOPE + B_ROPE
    idx, sign = _rot_half_cols()
    nope = np.concatenate([h * dq + np.arange(B_NOPE) for h in range(heads)])
    pe = np.concatenate([np.concatenate([h * dq + B_NOPE + np.arange(B_ROPE), h * dq + B_NOPE + idx])
                         for h in range(heads)])
    pe_sign = np.concatenate([np.concatenate([np.ones(B_ROPE), sign]) for _ in range(heads)])
    cols = np.concatenate([nope, pe])
    sg = np.concatenate([np.ones(nope.shape[0]), pe_sign])
    return w[:, cols] * jnp.asarray(sg, w.dtype)[None, :]


def kernel(x, meta_tokens, lb_logits, norm_g, ab_w_in, hgrn_norm_g, mla_q_norm_g, mla_kv_norm_g,
           mla_w_uq, mla_w_ukv, ab_w_out, cd_w_in, pool_w, pool_scale, ssm_conv_w, ssm_conv_b,
           ssm_dt_bias, ssm_a_log, ssm_d, ssm_norm_g, cd_w_out, ffn_w_up, ffn_conv_w, ffn_conv_b,
           ffn_w_down):
    bsz, seq, d = x.shape
    assert bsz == 1
    depth = norm_g.shape[0]
    total = N_META + seq
    lp = -(-total // Q_BLOCK) * Q_BLOCK
    a_heads = hgrn_norm_g.shape[1]
    a_kw = a_heads * A_DK
    a_w = a_heads * A_DV
    b_heads = mla_w_uq.shape[2] // (B_NOPE + B_ROPE)
    c_width = pool_scale.shape[1]
    d_inner = ssm_norm_g.shape[1]
    d_heads = ssm_dt_bias.shape[1]
    d_xbc = ssm_conv_w.shape[2]
    d_ff = ffn_w_down.shape[1]

    tm_mm = _pick(lp, (1040, 640, 512, 256, 128))
    tm_row = _pick(lp, (320, 256, 128))
    tb = _pick(lp, (320, 256, 128, 64))
    tq = _pick(lp, (640, 512, 256, 128))
    tn_ff = _pick(d_ff, (256, 128))

    h = jnp.concatenate([meta_tokens.astype(x.dtype), x[0],
                         jnp.zeros((lp - total, d), x.dtype)], axis=0)

    inv = ROPE_THETA ** (-jnp.arange(0, B_ROPE, 2, dtype=F32) / B_ROPE)
    ang = jnp.arange(lp, dtype=F32)[:, None] * inv[None, :]
    cos, sin = jnp.cos(ang), jnp.sin(ang)
    cs = jnp.concatenate([cos, cos, sin, sin], axis=-1)
    lb_all = jnp.cumsum(jax.nn.softmax(lb_logits.astype(F32), axis=0), axis=0)

    xn = _rownorm(h, norm_g[0, 0], width=d, tm=tm_row, name="norm_in")

    for layer in range(depth):
        g = norm_g[layer]
        j = layer // 2
        if layer % 2 == 0:
            w_in = _layout_w_in_ab(ab_w_in[j].astype(BF16), a_kw, a_w)
            n_ab = 2 * a_kw + 2 * a_w
            n_lora = B_Q_LORA + B_KV_LORA
            proj_a = _matmul(xn, w_in, k=d, col0=0, ncols=n_ab, tm=tm_mm, tn=512, out_dtype=F32, name="ab_in_a")
            proj_c = _matmul(xn, w_in, k=d, col0=n_ab, ncols=n_lora, tm=tm_mm, tn=512, out_dtype=F32, name="ab_in_c")
            krp = _matmul(xn, w_in, k=d, col0=n_ab + n_lora, ncols=LANES, tm=tm_mm, tn=LANES, out_dtype=F32,
                          name="ab_in_kr")
            o_a = _hgrn2(proj_a, lb_all[layer], hgrn_norm_g[j], heads=a_heads, tb=tb, name="hgrn2")
            cq = _rownorm(proj_c, mla_q_norm_g[j], width=B_Q_LORA, xblk=0, tm=tm_row, name="q_norm")
            ckv = _rownorm(proj_c, mla_kv_norm_g[j], width=B_KV_LORA, xblk=B_Q_LORA // B_KV_LORA, tm=tm_row,
                           name="kv_norm")
            w_uq = _layout_w_uq(mla_w_uq[j].astype(BF16), b_heads)
            qf = _matmul(cq, w_uq, k=B_Q_LORA, ncols=w_uq.shape[1], tm=tm_mm, tn=512, out_dtype=F32, name="uq")
            kvf = _matmul(ckv, mla_w_ukv[j].astype(BF16), k=B_KV_LORA, ncols=mla_w_ukv.shape[2], tm=tm_mm, tn=512,
                          out_dtype=BF16, name="ukv")
            kp = _kprep(krp, cs, tm=tm_row, name="k_rope")
            o_b = _attention(qf, cs, kvf, kp, heads=b_heads, tq=tq, tk=tq, name="attention")
            mix_in = jnp.concatenate([o_a, o_b], axis=-1)
            w_out = ab_w_out[j].astype(BF16)
        else:
            w = cd_w_in[j].astype(BF16)
            n_zu = d_inner + c_width
            dt_pad = LANES - d_heads
            w_in = jnp.concatenate([w[:, c_width:n_zu], w[:, :c_width], w[:, n_zu:],
                                    jnp.zeros((d, dt_pad), BF16)], axis=1)
            zu = _matmul(xn, w_in, k=d, col0=0, ncols=n_zu, tm=tm_mm, tn=512, out_dtype=F32, name="cd_in_zu")
            xbc = _matmul(xn, w_in, k=d, col0=n_zu, ncols=d_xbc, tm=tm_mm, tn=512, out_dtype=F32, name="cd_in_xbc")
            dt = _matmul(xn, w_in, k=d, col0=n_zu + d_xbc, ncols=LANES, tm=tm_mm, tn=LANES, out_dtype=F32,
                         name="cd_in_dt")
            o_c = _pool(zu, pool_w[j], pool_scale[j], ublk=d_inner // c_width, tm=tm_row, name="pool")
            xbc = _ssd_conv(xbc, ssm_conv_w[j], ssm_conv_b[j], tm=tm_row, tc=_pick(d_xbc, (1280, 1024, 512)),
                            name="ssd_conv")
            a_ch = jnp.repeat(-jnp.exp(ssm_a_log[j].astype(F32)), D_HEAD_DIM).reshape(1, d_inner)
            d_ch = jnp.repeat(ssm_d[j].astype(F32), D_HEAD_DIM).reshape(1, d_inner)
            dtb = jnp.pad(ssm_dt_bias[j].astype(F32), (0, dt_pad)).reshape(1, LANES)
            y = _ssd(xbc, dt, zu, dtb, a_ch, d_ch, heads=d_heads, tb=tb, name="ssd")
            yn = _rownorm(y, ssm_norm_g[j], width=d_inner, tm=tm_row, name="ssd_norm")
            mix_in = jnp.concatenate([o_c, yn], axis=-1)
            w_out = cd_w_out[j].astype(BF16)

        mixed = _matmul(mix_in, w_out, k=mix_in.shape[1], ncols=d, tm=tm_mm, tn=512, out_dtype=F32, name="mix_out")
        h, xn = _addnorm(h, mixed, g[1], g[2], tm=tm_row, name="add_norm_mix")
        act = _ffn_up(xn, ffn_w_up[layer].astype(BF16), ffn_conv_w[layer], ffn_conv_b[layer],
                      tm=tm_mm, tn=tn_ff, name="ffn_up")
        ff = _matmul(act, ffn_w_down[layer].astype(BF16), k=d_ff, ncols=d, tm=_pick(lp, (640, 512, 256, 128)),
                     tn=256, out_dtype=F32, name="ffn_down", vmem_mib=56)
        g_next = norm_g[layer + 1, 0] if layer + 1 < depth else None
        h, xn = _addnorm(h, ff, g[3], g_next, tm=tm_row, name="add_norm_ffn")

    return h[N_META:total][None]
```

```python
import functools
import math

import jax
import jax.numpy as jnp
from jax import lax
from jax.experimental import pallas as pl
from jax.experimental.pallas import tpu as pltpu

F32 = jnp.float32
BF16 = jnp.bfloat16
HIGHEST = lax.Precision.HIGHEST

CHUNK = 64
N_META = 16
Q_BLOCK = 128
NORM_EPS = 1e-6
A_DK = 128
A_DV = 128
B_NOPE = 128
B_ROPE = 64
B_DV = 128
B_Q_LORA = 1024
B_KV_LORA = 512
ROPE_THETA = 10000.0
POOL_WINDOWS = (2, 4, 8, 16)
D_HEAD_DIM = 64
D_GROUPS = 8
D_STATE = 128
D_CONV = 4
FFN_CONV = 3

SCAN_T = 64
SUB_T = 16
LANES = 128
HALO_BF16 = 16
HALO_F32 = 8
MIB = 1024 * 1024

NT_DIMS = (((1,), (1,)), ((), ()))
TN_DIMS = (((0,), (0,)), ((), ()))


def _cp(sem, vmem_mib=48):
    return pltpu.CompilerParams(dimension_semantics=sem, vmem_limit_bytes=vmem_mib * MIB)


def _pick(n, cands):
    for c in cands:
        if n % c == 0:
            return c
    raise ValueError(f"no tile for {n} in {cands}")


def _nchunks(rows, most=6):
    tiles = rows // HALO_BF16
    return max(c for c in range(1, most + 1) if tiles % c == 0)


def _rms(x, g):
    return x * lax.rsqrt(jnp.mean(x * x, axis=-1, keepdims=True) + NORM_EPS) * g


def _mm_body(*refs, n_lhs):
    o_ref = refs[-1]
    acc = None
    for x_ref, w_ref in zip(refs[:n_lhs], refs[n_lhs:2 * n_lhs]):
        part = jnp.dot(x_ref[...], w_ref[...].astype(BF16), preferred_element_type=F32)
        acc = part if acc is None else acc + part
    o_ref[...] = acc.astype(o_ref.dtype)


def _matmul(pieces, w, *, layer=0, col0=0, ncols, tm, tn, out_dtype, name, vmem_mib=56):
    rows = pieces[0][0].shape[0]
    assert col0 % tn == 0 and ncols % tn == 0 and rows % tm == 0
    cb = col0 // tn
    x_specs = [pl.BlockSpec((tm, k), lambda i, j, xb=xb: (i, xb)) for _, k, xb, _ in pieces]
    w_specs = [pl.BlockSpec((pl.Squeezed(), k, tn), lambda i, j, wb=wb: (layer, wb, cb + j))
               for _, k, _, wb in pieces]
    return pl.pallas_call(
        functools.partial(_mm_body, n_lhs=len(pieces)),
        grid=(rows // tm, ncols // tn),
        in_specs=x_specs + w_specs,
        out_specs=pl.BlockSpec((tm, tn), lambda i, j: (i, j)),
        out_shape=jax.ShapeDtypeStruct((rows, ncols), out_dtype),
        compiler_params=_cp(("parallel", "parallel"), vmem_mib),
        name=name,
    )(*[p[0] for p in pieces], *([w] * len(pieces)))


def _mm_nt_body(w_ref, x_ref, o_ref):
    o_ref[...] = lax.dot_general(w_ref[...], x_ref[...], NT_DIMS,
                                 preferred_element_type=F32).astype(o_ref.dtype)


def _matmul_nt(wt, x, *, tm, tn, name):
    n, k = wt.shape
    rows = x.shape[0]
    return pl.pallas_call(
        _mm_nt_body,
        grid=(n // tn, rows // tm),
        in_specs=[pl.BlockSpec((tn, k), lambda a, b: (a, 0)),
                  pl.BlockSpec((tm, k), lambda a, b: (b, 0))],
        out_specs=pl.BlockSpec((tn, tm), lambda a, b: (a, b)),
        out_shape=jax.ShapeDtypeStruct((n, rows), BF16),
        compiler_params=_cp(("parallel", "parallel")),
        name=name,
    )(wt, x)


def _rownorm_body(x_ref, g_ref, o_ref):
    o_ref[...] = _rms(x_ref[...].astype(F32), g_ref[...]).astype(o_ref.dtype)


def _rownorm(x, g, *, width, xblk=0, tm, name):
    rows = x.shape[0]
    return pl.pallas_call(
        _rownorm_body,
        grid=(rows // tm,),
        in_specs=[pl.BlockSpec((tm, width), lambda i: (i, xblk)),
                  pl.BlockSpec((1, width), lambda i: (0, 0))],
        out_specs=pl.BlockSpec((tm, width), lambda i: (i, 0)),
        out_shape=jax.ShapeDtypeStruct((rows, width), BF16),
        compiler_params=_cp(("parallel",)),
        name=name,
    )(x, g.reshape(1, width).astype(F32))


def _addnorm_body(h_ref, m_ref, g1_ref, g2_ref, ho_ref, xo_ref):
    hn = h_ref[...] + _rms(m_ref[...], g1_ref[...])
    ho_ref[...] = hn
    xo_ref[...] = _rms(hn, g2_ref[...]).astype(xo_ref.dtype)


def _add_body(h_ref, m_ref, g1_ref, ho_ref):
    ho_ref[...] = h_ref[...] + _rms(m_ref[...], g1_ref[...])


def _addnorm(h, m, g_post, g_next, *, tm, name):
    rows, d = h.shape
    row_spec = pl.BlockSpec((tm, d), lambda i: (i, 0))
    g_spec = pl.BlockSpec((1, d), lambda i: (0, 0))
    if g_next is None:
        return pl.pallas_call(
            _add_body, grid=(rows // tm,),
            in_specs=[row_spec, row_spec, g_spec], out_specs=row_spec,
            out_shape=jax.ShapeDtypeStruct((rows, d), F32),
            compiler_params=_cp(("parallel",)), name=name,
        )(h, m, g_post.reshape(1, d)), None
    return pl.pallas_call(
        _addnorm_body, grid=(rows // tm,),
        in_specs=[row_spec, row_spec, g_spec, g_spec], out_specs=[row_spec, row_spec],
        out_shape=[jax.ShapeDtypeStruct((rows, d), F32), jax.ShapeDtypeStruct((rows, d), BF16)],
        compiler_params=_cp(("parallel",)), name=name,
    )(h, m, g_post.reshape(1, d), g_next.reshape(1, d))


def _ffn_up_body(x_ref, halo_ref, wg_ref, wv_ref, cwg_ref, cwv_ref, cbg_ref, cbv_ref, o_ref,
                 xcat_ref, ua_ref, ub_ref, *, n_mm, n_ep):
    i = pl.program_id(0)
    j = pl.program_id(1)

    @pl.when(j == 0)
    def _():
        hl = halo_ref[...]
        xcat_ref[0:HALO_BF16, :] = jnp.where(i == 0, jnp.zeros_like(hl), hl)
        xcat_ref[HALO_BF16:, :] = x_ref[...]
        ub_ref[...] = jnp.zeros_like(ub_ref)

    rows_in = xcat_ref.shape[0]
    rows_out = o_ref.shape[0]
    mm_rows = rows_in // n_mm
    ep_rows = rows_out // n_ep

    def conv(prev_ref, which, lo, cw_ref, cb_ref):
        u = prev_ref[which, HALO_BF16 + lo - HALO_F32:HALO_BF16 + lo + ep_rows, :]
        cw = cw_ref[...]
        y = (cw[2:3, :] * u + cw[1:2, :] * pltpu.roll(u, 1, 0)
             + cw[0:1, :] * pltpu.roll(u, 2, 0) + cb_ref[...])
        return y[HALO_F32:, :]

    def stage(prev_ref, next_ref):
        wg = wg_ref[...].astype(BF16)
        wv = wv_ref[...].astype(BF16)
        for c in range(max(n_mm, n_ep)):
            if c < n_mm:
                xc = xcat_ref[c * mm_rows:(c + 1) * mm_rows, :]
                next_ref[0, c * mm_rows:(c + 1) * mm_rows, :] = jnp.dot(xc, wg, preferred_element_type=F32)
                next_ref[1, c * mm_rows:(c + 1) * mm_rows, :] = jnp.dot(xc, wv, preferred_element_type=F32)
            if c < n_ep:
                lo = c * ep_rows
                gate = conv(prev_ref, 0, lo, cwg_ref, cbg_ref)
                val = conv(prev_ref, 1, lo, cwv_ref, cbv_ref)
                o_ref[lo:lo + ep_rows, :] = (jax.nn.silu(gate) * val).astype(o_ref.dtype)

    @pl.when(j % 2 == 0)
    def _():
        stage(ub_ref, ua_ref)

    @pl.when(j % 2 == 1)
    def _():
        stage(ua_ref, ub_ref)


def _ffn_up(xn, w_up, conv_w, conv_b, *, layer, tm, tn, name):
    rows, d = xn.shape
    d_ff = w_up.shape[2] // 2
    nj = d_ff // tn
    hb = tm // HALO_BF16
    cb = conv_b.reshape(conv_b.shape[0], 1, 2 * d_ff)
    cur = lambda j: jnp.minimum(j, nj - 1)
    prev = lambda j: jnp.maximum(j - 1, 0)
    sq = pl.Squeezed()
    return pl.pallas_call(
        functools.partial(_ffn_up_body, n_mm=_nchunks(tm + HALO_BF16), n_ep=_nchunks(tm)),
        grid=(rows // tm, nj + 1),
        in_specs=[pl.BlockSpec((tm, d), lambda i, j: (i, 0)),
                  pl.BlockSpec((HALO_BF16, d), lambda i, j: (jnp.maximum(i * hb - 1, 0), 0)),
                  pl.BlockSpec((sq, d, tn), lambda i, j: (layer, 0, cur(j))),
                  pl.BlockSpec((sq, d, tn), lambda i, j: (layer, 0, nj + cur(j))),
                  pl.BlockSpec((sq, FFN_CONV, tn), lambda i, j: (layer, 0, prev(j))),
                  pl.BlockSpec((sq, FFN_CONV, tn), lambda i, j: (layer, 0, nj + prev(j))),
                  pl.BlockSpec((sq, 1, tn), lambda i, j: (layer, 0, prev(j))),
                  pl.BlockSpec((sq, 1, tn), lambda i, j: (layer, 0, nj + prev(j)))],
        out_specs=pl.BlockSpec((tm, tn), lambda i, j: (i, prev(j))),
        out_shape=jax.ShapeDtypeStruct((rows, d_ff), BF16),
        scratch_shapes=[pltpu.VMEM((tm + HALO_BF16, d), BF16),
                        pltpu.VMEM((2, tm + HALO_BF16, tn), F32),
                        pltpu.VMEM((2, tm + HALO_BF16, tn), F32)],
        compiler_params=_cp(("parallel", "arbitrary"), 56),
        name=name,
    )(xn, xn, w_up, w_up, conv_w, conv_w, cb, cb)


def _hgrn2_body(q_ref, f_ref, v_ref, g_ref, lb_ref, hg_ref, o_ref, st_ref, *, nsub):
    @pl.when(pl.program_id(1) == 0)
    def _():
        st_ref[...] = jnp.zeros_like(st_ref)

    t = SCAN_T
    lb = lb_ref[...]
    hg = hg_ref[...]
    ri = lax.broadcasted_iota(jnp.int32, (t, t), 0)
    ci = lax.broadcasted_iota(jnp.int32, (t, t), 1)
    tril = ci <= ri
    trilf = tril.astype(F32)
    jrow = lax.broadcasted_iota(jnp.int32, (t, A_DK), 0)

    for s in range(nsub):
        sl = slice(s * t, (s + 1) * t)
        q = q_ref[sl, :]
        fp = f_ref[sl, :]
        v = v_ref[sl, :].astype(BF16)
        lf = jnp.log(lb + (1.0 - lb) * jax.nn.sigmoid(fp))
        k = (1.0 - lb) * jax.nn.sigmoid(-fp)
        b = jnp.dot(trilf, lf, precision=HIGHEST, preferred_element_type=F32)
        st = st_ref[...]
        o = lax.dot_general((q * jnp.exp(b)).astype(BF16), st.astype(BF16), NT_DIMS,
                            preferred_element_type=F32)
        rows = []
        for blk in range(t // SUB_T):
            lo = blk * SUB_T
            r = b[lo + SUB_T // 2 - 1:lo + SUB_T // 2, :]
            qt = (q[lo:lo + SUB_T, :] * jnp.exp(b[lo:lo + SUB_T, :] - r)).astype(BF16)
            kh = jnp.where(jrow < lo + SUB_T, k * jnp.exp(r - b), 0.0).astype(BF16)
            rows.append(lax.dot_general(qt, kh, NT_DIMS, preferred_element_type=F32))
        a = jnp.where(tril, jnp.concatenate(rows, axis=0), 0.0)
        o = o + jnp.dot(a.astype(BF16), v, preferred_element_type=F32)
        bl = b[t - 1:t, :]
        kd = (k * jnp.exp(bl - b)).astype(BF16)
        st_ref[...] = st * jnp.exp(bl) + lax.dot_general(v, kd, TN_DIMS, preferred_element_type=F32)
        o = _rms(o, hg)
        o_ref[sl, :] = (o * jax.nn.silu(g_ref[sl, :])).astype(o_ref.dtype)


def _hgrn2(proj, lb, hg, *, heads, tb, name):
    rows = proj.shape[0]
    blk = lambda off: pl.BlockSpec((tb, A_DK), lambda h, c, off=off: (c, off + h))
    vec = pl.BlockSpec((pl.Squeezed(), 1, A_DK), lambda h, c: (h, 0, 0))
    return pl.pallas_call(
        functools.partial(_hgrn2_body, nsub=tb // SCAN_T),
        grid=(heads, rows // tb),
        in_specs=[blk(0), blk(heads), blk(2 * heads), blk(3 * heads), vec, vec],
        out_specs=pl.BlockSpec((tb, A_DV), lambda h, c: (c, h)),
        out_shape=jax.ShapeDtypeStruct((rows, heads * A_DV), BF16),
        scratch_shapes=[pltpu.VMEM((A_DV, A_DK), F32)],
        compiler_params=_cp(("parallel", "arbitrary")),
        name=name,
    )(proj, proj, proj, proj, lb.reshape(heads, 1, A_DK), hg.reshape(heads, 1, A_DV))


def _kprep_body(kr_ref, cs_ref, o_ref):
    t = kr_ref[...] * cs_ref[...]
    o_ref[...] = (t + pltpu.roll(t, B_ROPE, 1)).astype(o_ref.dtype)


def _kprep(krp, cs, *, tm, name):
    rows = krp.shape[0]
    spec = pl.BlockSpec((tm, LANES), lambda i: (i, 0))
    return pl.pallas_call(
        _kprep_body, grid=(rows // tm,), in_specs=[spec, spec], out_specs=spec,
        out_shape=jax.ShapeDtypeStruct((rows, LANES), BF16),
        compiler_params=_cp(("parallel",)), name=name,
    )(krp, cs)


def _attn_body(qn_ref, qp_ref, cs_ref, kn_ref, kp_ref, vt_ref, o_ref, kcat_ref, *, tq, tk, n_kv, scale):
    i = pl.program_id(1)

    @pl.when(i == 0)
    def _():
        kcat_ref[:, 0:B_NOPE] = kn_ref[...]
        kcat_ref[:, B_NOPE:] = kp_ref[...]

    qs = i * tq
    qcat = jnp.concatenate([(qn_ref[...] * scale).astype(BF16),
                            (qp_ref[...] * cs_ref[...] * scale).astype(BF16)], axis=1)
    off = CHUNK - N_META
    first_cid = (qs + off) // CHUNK
    last_cid = (qs + tq - 1 + off) // CHUNK
    n_full = jnp.minimum((CHUNK * first_cid + N_META) // tk, n_kv)
    n_vis = jnp.minimum((CHUNK * last_cid + N_META - 1) // tk + 1, n_kv)
    q_cid = (qs + lax.broadcasted_iota(jnp.int32, (tk, tq), 1) + off) // CHUNK
    k_iota = lax.broadcasted_iota(jnp.int32, (tk, tq), 0) + off

    def make_step(masked):
        def step(j, carry):
            m, l, acc = carry
            ks = pl.multiple_of(j * tk, tk)
            s = lax.dot_general(kcat_ref[pl.ds(ks, tk), :], qcat, NT_DIMS, preferred_element_type=F32)
            if masked:
                s = jnp.where((k_iota + ks) // CHUNK <= q_cid, s, -jnp.inf)
            m_new = jnp.maximum(m, jnp.max(s, axis=0, keepdims=True))
            p = jnp.exp2(s - m_new)
            alpha = jnp.exp2(m - m_new)
            l = alpha * l + jnp.sum(p, axis=0, keepdims=True)
            acc = alpha * acc + jnp.dot(vt_ref[:, pl.ds(ks, tk)], p.astype(BF16), preferred_element_type=F32)
            return m_new, l, acc
        return step

    init = (jnp.full((1, tq), -jnp.inf, F32), jnp.zeros((1, tq), F32), jnp.zeros((B_DV, tq), F32))
    carry = lax.fori_loop(0, n_full, make_step(False), init)
    _, l, acc = lax.fori_loop(n_full, n_vis, make_step(True), carry)
    o_ref[...] = (acc / l).T.astype(o_ref.dtype)


def _attention(qf, cs, kn, kp, vt, *, heads, tq, tk, name):
    rows = qf.shape[0]
    scale = float(B_NOPE + B_ROPE) ** -0.5 * math.log2(math.e)
    return pl.pallas_call(
        functools.partial(_attn_body, tq=tq, tk=tk, n_kv=rows // tk, scale=scale),
        grid=(heads, rows // tq),
        in_specs=[pl.BlockSpec((tq, LANES), lambda h, i: (i, h)),
                  pl.BlockSpec((tq, LANES), lambda h, i: (i, heads + h)),
                  pl.BlockSpec((tq, LANES), lambda h, i: (i, 0)),
                  pl.BlockSpec((rows, B_NOPE), lambda h, i: (0, h)),
                  pl.BlockSpec((rows, LANES), lambda h, i: (0, 0)),
                  pl.BlockSpec((B_DV, rows), lambda h, i: (h, 0))],
        out_specs=pl.BlockSpec((tq, B_DV), lambda h, i: (i, h)),
        out_shape=jax.ShapeDtypeStruct((rows, heads * B_DV), BF16),
        scratch_shapes=[pltpu.VMEM((rows, B_NOPE + LANES), BF16)],
        compiler_params=_cp(("parallel", "arbitrary")),
        name=name,
    )(qf, qf, cs, kn, kp, vt)


def _pool_body(u_ref, halo_ref, w_ref, sc_ref, o_ref, *, tm, gdim):
    i = pl.program_id(0)
    hl = halo_ref[...]
    hl = jnp.where(i == 0, jnp.zeros_like(hl), hl)
    halo = hl.shape[0]
    pos = i * tm + lax.broadcasted_iota(jnp.int32, (tm, gdim), 0)
    for gi, win in enumerate(POOL_WINDOWS):
        cols = slice(gi * gdim, (gi + 1) * gdim)
        x = u_ref[:, cols]
        s = jnp.concatenate([hl[:, cols], x], axis=0)
        sh = 1
        while sh < win:
            s = s + pltpu.roll(s, sh, 0)
            sh *= 2
        cnt = jnp.minimum(pos + 1, win).astype(F32)
        pooled = s[halo:, :] / cnt - x
        y = jnp.dot(pooled.astype(BF16), w_ref[gi], preferred_element_type=F32)
        o_ref[:, cols] = (y * sc_ref[:, cols]).astype(o_ref.dtype)


def _pool(u, pool_w, pool_scale, *, tm, name):
    rows = u.shape[0]
    groups, gdim, _ = pool_w.shape
    cw = groups * gdim
    halo = 2 * HALO_F32
    hb = tm // halo
    return pl.pallas_call(
        functools.partial(_pool_body, tm=tm, gdim=gdim),
        grid=(rows // tm,),
        in_specs=[pl.BlockSpec((tm, cw), lambda i: (i, 0)),
                  pl.BlockSpec((halo, cw), lambda i: (jnp.maximum(i * hb - 1, 0), 0)),
                  pl.BlockSpec((groups, gdim, gdim), lambda i: (0, 0, 0)),
                  pl.BlockSpec((1, cw), lambda i: (0, 0))],
        out_specs=pl.BlockSpec((tm, cw), lambda i: (i, 0)),
        out_shape=jax.ShapeDtypeStruct((rows, cw), BF16),
        compiler_params=_cp(("parallel",)),
        name=name,
    )(u, u, pool_w.astype(BF16), pool_scale.reshape(1, cw))


def _ssd_conv_body(x_ref, halo_ref, cw_ref, cb_ref, o_ref):
    i = pl.program_id(0)
    hl = halo_ref[...]
    hl = jnp.where(i == 0, jnp.zeros_like(hl), hl)
    halo = hl.shape[0]
    x = jnp.concatenate([hl, x_ref[...]], axis=0)
    cw = cw_ref[...]
    y = cw[D_CONV - 1:D_CONV, :] * x + cb_ref[...]
    for kk in range(1, D_CONV):
        y = y + cw[D_CONV - 1 - kk:D_CONV - kk, :] * pltpu.roll(x, kk, 0)
    o_ref[...] = jax.nn.silu(y[halo:, :])


def _ssd_conv(xbc, conv_w, conv_b, *, tm, tc, name):
    rows, ch = xbc.shape
    hb = tm // HALO_F32
    return pl.pallas_call(
        _ssd_conv_body,
        grid=(rows // tm, ch // tc),
        in_specs=[pl.BlockSpec((tm, tc), lambda i, j: (i, j)),
                  pl.BlockSpec((HALO_F32, tc), lambda i, j: (jnp.maximum(i * hb - 1, 0), j)),
                  pl.BlockSpec((D_CONV, tc), lambda i, j: (0, j)),
                  pl.BlockSpec((1, tc), lambda i, j: (0, j))],
        out_specs=pl.BlockSpec((tm, tc), lambda i, j: (i, j)),
        out_shape=jax.ShapeDtypeStruct((rows, ch), F32),
        compiler_params=_cp(("parallel", "parallel")),
        name=name,
    )(xbc, xbc, conv_w, conv_b.reshape(1, ch))


def _ssd_body(xs_ref, b_ref, c_ref, dt_ref, z_ref, dtb_ref, a_ref, dsk_ref, o_ref, st_ref, *, nsub, hpg):
    g = pl.program_id(0)

    @pl.when(pl.program_id(1) == 0)
    def _():
        st_ref[...] = jnp.zeros_like(st_ref)

    t = SCAN_T
    gw = hpg * D_HEAD_DIM
    a_row = a_ref[...]
    dsk = dsk_ref[...]
    dtb = dtb_ref[...]
    hrow = lax.broadcasted_iota(jnp.int32, (LANES, gw), 0)
    hcol = lax.broadcasted_iota(jnp.int32, (LANES, gw), 1) // D_HEAD_DIM + g * hpg
    sel = (hrow == hcol).astype(F32)
    ri = lax.broadcasted_iota(jnp.int32, (t, t), 0)
    ci = lax.broadcasted_iota(jnp.int32, (t, t), 1)
    trilf = (ci <= ri).astype(F32)
    pr = lax.broadcasted_iota(jnp.int32, (t, 2 * t), 0)
    pc = lax.broadcasted_iota(jnp.int32, (t, 2 * t), 1)
    pj = pc % t
    tril2 = pj <= pr
    triu2f = (pr <= pj).astype(F32)
    left = pc < t

    for s in range(nsub):
        sl = slice(s * t, (s + 1) * t)
        xs = xs_ref[sl, :]
        bb = b_ref[sl, :].astype(BF16)
        cc = c_ref[sl, :].astype(BF16)
        dt = jax.nn.softplus(dt_ref[sl, :] + dtb)
        dtx = jnp.dot(dt, sel, precision=HIGHEST, preferred_element_type=F32)
        la = dtx * a_row
        acs = jnp.dot(trilf, la, precision=HIGHEST, preferred_element_type=F32)
        xd = xs * dtx
        last = acs[t - 1:t, :]
        st = st_ref[...]
        y = jnp.exp(acs) * jnp.dot(cc, st.astype(BF16), preferred_element_type=F32)
        st_ref[...] = st * jnp.exp(last) + lax.dot_general(
            bb, (xd * jnp.exp(last - acs)).astype(BF16), TN_DIMS, preferred_element_type=F32)
        cb2 = lax.dot_general(cc, jnp.concatenate([bb, bb], axis=0), NT_DIMS, preferred_element_type=F32)
        diag = []
        for p in range(hpg // 2):
            cols = slice(p * 2 * D_HEAD_DIM, (p + 1) * 2 * D_HEAD_DIM)
            row_cs = jnp.sum(la[:, cols] * triu2f, axis=0, keepdims=True)
            lmat = jnp.where(tril2, jnp.exp(acs[:, cols] - row_cs), 0.0)
            xdp = xd[:, cols]
            rhs = jnp.concatenate([jnp.where(left, xdp, 0.0), jnp.where(left, 0.0, xdp)], axis=0)
            diag.append(jnp.dot((cb2 * lmat).astype(BF16), rhs.astype(BF16), preferred_element_type=F32))
        y = y + jnp.concatenate(diag, axis=1) + xs * dsk
        o_ref[sl, :] = y * jax.nn.silu(z_ref[sl, :])


def _ssd(xbc, dt, z, dt_bias, a_ch, d_ch, *, heads, tb, name):
    rows = xbc.shape[0]
    inner = heads * D_HEAD_DIM
    hpg = heads // D_GROUPS
    gw = hpg * D_HEAD_DIM
    assert hpg % 2 == 0 and 2 * D_HEAD_DIM == LANES and SCAN_T == D_HEAD_DIM
    nxb = inner // LANES
    return pl.pallas_call(
        functools.partial(_ssd_body, nsub=tb // SCAN_T, hpg=hpg),
        grid=(D_GROUPS, rows // tb),
        in_specs=[pl.BlockSpec((tb, gw), lambda g, c: (c, g)),
                  pl.BlockSpec((tb, D_STATE), lambda g, c: (c, nxb + g)),
                  pl.BlockSpec((tb, D_STATE), lambda g, c: (c, nxb + D_GROUPS + g)),
                  pl.BlockSpec((tb, LANES), lambda g, c: (c, 0)),
                  pl.BlockSpec((tb, gw), lambda g, c: (c, g)),
                  pl.BlockSpec((1, LANES), lambda g, c: (0, 0)),
                  pl.BlockSpec((1, gw), lambda g, c: (0, g)),
                  pl.BlockSpec((1, gw), lambda g, c: (0, g))],
        out_specs=pl.BlockSpec((tb, gw), lambda g, c: (c, g)),
        out_shape=jax.ShapeDtypeStruct((rows, inner), F32),
        scratch_shapes=[pltpu.VMEM((D_STATE, gw), F32)],
        compiler_params=_cp(("parallel", "arbitrary")),
        name=name,
    )(xbc, xbc, xbc, dt, z, dt_bias, a_ch, d_ch)


def _with_rot_half(w):
    half = B_ROPE // 2
    return jnp.concatenate([w, -w[..., half:], w[..., :half]], axis=-1)


def _layout_w_uq(w, heads):
    k = w.shape[0]
    w3 = w.astype(BF16).reshape(k, heads, B_NOPE + B_ROPE)
    nope = w3[:, :, :B_NOPE].reshape(k, heads * B_NOPE)
    pe = _with_rot_half(w3[:, :, B_NOPE:]).reshape(k, heads * LANES)
    return jnp.concatenate([nope, pe], axis=1)[None]


def kernel(x, meta_tokens, lb_logits, norm_g, ab_w_in, hgrn_norm_g, mla_q_norm_g, mla_kv_norm_g,
           mla_w_uq, mla_w_ukv, ab_w_out, cd_w_in, pool_w, pool_scale, ssm_conv_w, ssm_conv_b,
           ssm_dt_bias, ssm_a_log, ssm_d, ssm_norm_g, cd_w_out, ffn_w_up, ffn_conv_w, ffn_conv_b,
           ffn_w_down):
    bsz, seq, d = x.shape
    assert bsz == 1
    depth = norm_g.shape[0]
    total = N_META + seq
    lp = -(-total // Q_BLOCK) * Q_BLOCK
    a_heads = hgrn_norm_g.shape[1]
    a_kw = a_heads * A_DK
    a_w = a_heads * A_DV
    b_heads = mla_w_uq.shape[2] // (B_NOPE + B_ROPE)
    c_width = pool_scale.shape[1]
    d_inner = ssm_norm_g.shape[1]
    d_heads = ssm_dt_bias.shape[1]
    d_xbc = ssm_conv_w.shape[2]
    d_ff = ffn_w_down.shape[1]

    tm_mm = _pick(lp, (1040, 640, 512, 256, 128))
    tm_mid = _pick(lp, (640, 512, 256, 128))
    tm_row = _pick(lp, (320, 256, 128))
    tb = _pick(lp, (320, 256, 128, 64))
    tq = _pick(lp, (640, 512, 256, 128))
    tn_ff = _pick(d_ff, (256, 128))

    h = jnp.concatenate([meta_tokens.astype(x.dtype), x[0],
                         jnp.zeros((lp - total, d), x.dtype)], axis=0)

    inv = ROPE_THETA ** (-jnp.arange(0, B_ROPE, 2, dtype=F32) / B_ROPE)
    ang = jnp.arange(lp, dtype=F32)[:, None] * inv[None, :]
    cos, sin = jnp.cos(ang), jnp.sin(ang)
    cs = jnp.concatenate([cos, cos, sin, sin], axis=-1)
    lb_all = jnp.cumsum(jax.nn.softmax(lb_logits.astype(F32), axis=0), axis=0)
    w_down = ffn_w_down.astype(BF16)

    xn = _rownorm(h, norm_g[0, 0], width=d, tm=tm_row, name="norm_in")

    for layer in range(depth):
        g = norm_g[layer]
        j = layer // 2
        xp = [(xn, d, 0, 0)]
        if layer % 2 == 0:
            n_ab = 2 * a_kw + 2 * a_w
            n_lora = B_Q_LORA + B_KV_LORA
            proj_a = _matmul(xp, ab_w_in, layer=j, col0=0, ncols=n_ab, tm=tm_mm, tn=512, out_dtype=F32,
                             name="ab_in_a")
            proj_c = _matmul(xp, ab_w_in, layer=j, col0=n_ab, ncols=n_lora, tm=tm_mm, tn=512, out_dtype=F32,
                             name="ab_in_c")
            w_kr = _with_rot_half(ab_w_in[j, :, n_ab + n_lora:].astype(BF16))[None]
            krp = _matmul(xp, w_kr, ncols=LANES, tm=tm_mm, tn=LANES, out_dtype=F32, name="ab_in_kr")
            o_a = _hgrn2(proj_a, lb_all[layer], hgrn_norm_g[j], heads=a_heads, tb=tb, name="hgrn2")
            cq = _rownorm(proj_c, mla_q_norm_g[j], width=B_Q_LORA, xblk=0, tm=tm_row, name="q_norm")
            ckv = _rownorm(proj_c, mla_kv_norm_g[j], width=B_KV_LORA, xblk=B_Q_LORA // B_KV_LORA, tm=tm_row,
                           name="kv_norm")
            w_uq = _layout_w_uq(mla_w_uq[j], b_heads)
            qf = _matmul([(cq, B_Q_LORA, 0, 0)], w_uq, ncols=w_uq.shape[2], tm=tm_mm, tn=512, out_dtype=F32,
                         name="uq")
            w_kv = mla_w_ukv[j].astype(BF16).reshape(B_KV_LORA, b_heads, 2, B_NOPE)
            w_k = w_kv[:, :, 0, :].reshape(1, B_KV_LORA, b_heads * B_NOPE)
            w_vt = w_kv[:, :, 1, :].reshape(B_KV_LORA, b_heads * B_DV).T
            kn = _matmul([(ckv, B_KV_LORA, 0, 0)], w_k, ncols=b_heads * B_NOPE, tm=tm_mm, tn=512,
                         out_dtype=BF16, name="uk")
            vt = _matmul_nt(w_vt, ckv, tm=tm_mid, tn=512, name="uv_t")
            kp = _kprep(krp, cs, tm=tm_row, name="k_rope")
            o_b = _attention(qf, cs, kn, kp, vt, heads=b_heads, tq=tq, tk=tq, name="attention")
            half = a_w
            mix = [(o_a, half, 0, 0), (o_b, half, 0, 1)]
            w_out, w_out_layer = ab_w_out, j
        else:
            n_zu = d_inner + c_width
            u_c = _matmul(xp, cd_w_in, layer=j, col0=0, ncols=c_width, tm=tm_mm, tn=512, out_dtype=F32,
                          name="cd_in_u")
            z = _matmul(xp, cd_w_in, layer=j, col0=c_width, ncols=d_inner, tm=tm_mm, tn=512, out_dtype=F32,
                        name="cd_in_z")
            xbc = _matmul(xp, cd_w_in, layer=j, col0=n_zu, ncols=d_xbc, tm=tm_mm, tn=512, out_dtype=F32,
                          name="cd_in_xbc")
            dt_pad = LANES - d_heads
            w_dt = jnp.pad(cd_w_in[j, :, n_zu + d_xbc:].astype(BF16), ((0, 0), (0, dt_pad)))[None]
            dt = _matmul(xp, w_dt, ncols=LANES, tm=tm_mm, tn=LANES, out_dtype=F32, name="cd_in_dt")
            o_c = _pool(u_c, pool_w[j], pool_scale[j], tm=tm_row, name="pool")
            xbc = _ssd_conv(xbc, ssm_conv_w[j], ssm_conv_b[j], tm=tm_row, tc=_pick(d_xbc, (1280, 1024, 512)),
                            name="ssd_conv")
            a_ch = jnp.repeat(-jnp.exp(ssm_a_log[j].astype(F32)), D_HEAD_DIM).reshape(1, d_inner)
            d_ch = jnp.repeat(ssm_d[j].astype(F32), D_HEAD_DIM).reshape(1, d_inner)
            dtb = jnp.pad(ssm_dt_bias[j].astype(F32), (0, dt_pad)).reshape(1, LANES)
            y = _ssd(xbc, dt, z, dtb, a_ch, d_ch, heads=d_heads, tb=tb, name="ssd")
            yn = _rownorm(y, ssm_norm_g[j], width=d_inner, tm=tm_row, name="ssd_norm")
            mix = [(o_c, c_width, 0, 0)] + [(yn, c_width, b, b + 1) for b in range(d_inner // c_width)]
            w_out, w_out_layer = cd_w_out, j

        mixed = _matmul(mix, w_out, layer=w_out_layer, ncols=d, tm=tm_mm, tn=512, out_dtype=F32, name="mix_out")
        h, xn = _addnorm(h, mixed, g[1], g[2], tm=tm_row, name="add_norm_mix")
        act = _ffn_up(xn, ffn_w_up, ffn_conv_w, ffn_conv_b, layer=layer, tm=tm_mm, tn=tn_ff, name="ffn_up")
        ff = _matmul([(act, d_ff, 0, 0)], w_down, layer=layer, ncols=d, tm=tm_mid, tn=256, out_dtype=F32,
                     name="ffn_down")
        g_next = norm_g[layer + 1, 0] if layer + 1 < depth else None
        h, xn = _addnorm(h, ff, g[3], g_next, tm=tm_row, name="add_norm_ffn")

    return h[N_META:total][None]
```

```python
import functools
import math

import jax
import jax.numpy as jnp
from jax import lax
from jax.experimental import pallas as pl
from jax.experimental.pallas import tpu as pltpu

F32 = jnp.float32
BF16 = jnp.bfloat16
HIGHEST = lax.Precision.HIGHEST

CHUNK = 64
N_META = 16
Q_BLOCK = 128
NORM_EPS = 1e-6
A_DK = 128
A_DV = 128
B_NOPE = 128
B_ROPE = 64
B_DV = 128
B_Q_LORA = 1024
B_KV_LORA = 512
ROPE_THETA = 10000.0
POOL_WINDOWS = (2, 4, 8, 16)
D_HEAD_DIM = 64
D_GROUPS = 8
D_STATE = 128
D_CONV = 4
FFN_CONV = 3

SCAN_T = 64
SUB_T = 16
LANES = 128
HALO_BF16 = 16
HALO_F32 = 8
MIB = 1024 * 1024

NT_DIMS = (((1,), (1,)), ((), ()))
TN_DIMS = (((0,), (0,)), ((), ()))


def _cp(sem, vmem_mib=48):
    return pltpu.CompilerParams(dimension_semantics=sem, vmem_limit_bytes=vmem_mib * MIB)


def _pick(n, cands):
    for c in cands:
        if n % c == 0:
            return c
    raise ValueError(f"no tile for {n} in {cands}")


def _nchunks(rows, most=6):
    tiles = rows // HALO_BF16
    return max(c for c in range(1, most + 1) if tiles % c == 0)


def _rms(x, g):
    return x * lax.rsqrt(jnp.mean(x * x, axis=-1, keepdims=True) + NORM_EPS) * g


def _mm_body(*refs, n_lhs):
    o_ref = refs[-1]
    acc = None
    for x_ref, w_ref in zip(refs[:n_lhs], refs[n_lhs:2 * n_lhs]):
        part = jnp.dot(x_ref[...], w_ref[...].astype(BF16), preferred_element_type=F32)
        acc = part if acc is None else acc + part
    o_ref[...] = acc.astype(o_ref.dtype)


def _matmul(pieces, w, *, layer=0, col0=0, ncols, tm, tn, out_dtype, name, vmem_mib=56):
    rows = pieces[0][0].shape[0]
    assert col0 % tn == 0 and ncols % tn == 0 and rows % tm == 0
    cb = col0 // tn
    x_specs = [pl.BlockSpec((tm, k), lambda i, j, xb=xb: (i, xb)) for _, k, xb, _ in pieces]
    w_specs = [pl.BlockSpec((pl.Squeezed(), k, tn), lambda i, j, wb=wb: (layer, wb, cb + j))
               for _, k, _, wb in pieces]
    return pl.pallas_call(
        functools.partial(_mm_body, n_lhs=len(pieces)),
        grid=(rows // tm, ncols // tn),
        in_specs=x_specs + w_specs,
        out_specs=pl.BlockSpec((tm, tn), lambda i, j: (i, j)),
        out_shape=jax.ShapeDtypeStruct((rows, ncols), out_dtype),
        compiler_params=_cp(("parallel", "parallel"), vmem_mib),
        name=name,
    )(*[p[0] for p in pieces], *([w] * len(pieces)))


def _mm_nt_body(w_ref, x_ref, o_ref):
    o_ref[...] = lax.dot_general(w_ref[...], x_ref[...], NT_DIMS,
                                 preferred_element_type=F32).astype(o_ref.dtype)


def _matmul_nt(wt, x, *, tm, tn, name):
    n, k = wt.shape
    rows = x.shape[0]
    return pl.pallas_call(
        _mm_nt_body,
        grid=(n // tn, rows // tm),
        in_specs=[pl.BlockSpec((tn, k), lambda a, b: (a, 0)),
                  pl.BlockSpec((tm, k), lambda a, b: (b, 0))],
        out_specs=pl.BlockSpec((tn, tm), lambda a, b: (a, b)),
        out_shape=jax.ShapeDtypeStruct((n, rows), BF16),
        compiler_params=_cp(("parallel", "parallel")),
        name=name,
    )(wt, x)


def _rownorm_body(x_ref, g_ref, o_ref):
    o_ref[...] = _rms(x_ref[...].astype(F32), g_ref[...]).astype(o_ref.dtype)


def _rownorm(x, g, *, width, xblk=0, tm, name):
    rows = x.shape[0]
    return pl.pallas_call(
        _rownorm_body,
        grid=(rows // tm,),
        in_specs=[pl.BlockSpec((tm, width), lambda i: (i, xblk)),
                  pl.BlockSpec((1, width), lambda i: (0, 0))],
        out_specs=pl.BlockSpec((tm, width), lambda i: (i, 0)),
        out_shape=jax.ShapeDtypeStruct((rows, width), BF16),
        compiler_params=_cp(("parallel",)),
        name=name,
    )(x, g.reshape(1, width).astype(F32))


def _addnorm_body(h_ref, m_ref, g1_ref, g2_ref, ho_ref, xo_ref):
    hn = h_ref[...] + _rms(m_ref[...], g1_ref[...])
    ho_ref[...] = hn
    xo_ref[...] = _rms(hn, g2_ref[...]).astype(xo_ref.dtype)


def _add_body(h_ref, m_ref, g1_ref, ho_ref):
    ho_ref[...] = h_ref[...] + _rms(m_ref[...], g1_ref[...])


def _add_rows(h, m, g_post, *, row0, nrows, tm, name):
    d = h.shape[1]
    assert row0 % HALO_F32 == 0 and tm % HALO_F32 == 0
    win = pl.BlockSpec((pl.Element(tm), pl.Element(d)),
                       lambda i: (pl.multiple_of(row0 + i * tm, HALO_F32), 0))
    return pl.pallas_call(
        _add_body, grid=(nrows // tm,),
        in_specs=[win, win, pl.BlockSpec((1, d), lambda i: (0, 0))],
        out_specs=pl.BlockSpec((tm, d), lambda i: (i, 0)),
        out_shape=jax.ShapeDtypeStruct((nrows, d), F32),
        compiler_params=_cp(("parallel",)), name=name,
    )(h, m, g_post.reshape(1, d))


def _addnorm(h, m, g_post, g_next, *, tm, name):
    rows, d = h.shape
    row_spec = pl.BlockSpec((tm, d), lambda i: (i, 0))
    g_spec = pl.BlockSpec((1, d), lambda i: (0, 0))
    return pl.pallas_call(
        _addnorm_body, grid=(rows // tm,),
        in_specs=[row_spec, row_spec, g_spec, g_spec], out_specs=[row_spec, row_spec],
        out_shape=[jax.ShapeDtypeStruct((rows, d), F32), jax.ShapeDtypeStruct((rows, d), BF16)],
        compiler_params=_cp(("parallel",)), name=name,
    )(h, m, g_post.reshape(1, d), g_next.reshape(1, d))


def _ffn_up_body(x_ref, halo_ref, wg_ref, wv_ref, cwg_ref, cwv_ref, cbg_ref, cbv_ref, o_ref,
                 xcat_ref, ua_ref, ub_ref, *, n_mm, n_ep):
    i = pl.program_id(0)
    j = pl.program_id(1)

    @pl.when(j == 0)
    def _():
        hl = halo_ref[...]
        xcat_ref[0:HALO_BF16, :] = jnp.where(i == 0, jnp.zeros_like(hl), hl)
        xcat_ref[HALO_BF16:, :] = x_ref[...]
        ub_ref[...] = jnp.zeros_like(ub_ref)

    rows_in = xcat_ref.shape[0]
    rows_out = o_ref.shape[0]
    mm_rows = rows_in // n_mm
    ep_rows = rows_out // n_ep

    def conv(prev_ref, which, lo, cw_ref, cb_ref):
        u = prev_ref[which, HALO_BF16 + lo - HALO_F32:HALO_BF16 + lo + ep_rows, :]
        cw = cw_ref[...]
        y = (cw[2:3, :] * u + cw[1:2, :] * pltpu.roll(u, 1, 0)
             + cw[0:1, :] * pltpu.roll(u, 2, 0) + cb_ref[...])
        return y[HALO_F32:, :]

    def stage(prev_ref, next_ref):
        wg = wg_ref[...].astype(BF16)
        wv = wv_ref[...].astype(BF16)
        for c in range(max(n_mm, n_ep)):
            if c < n_mm:
                xc = xcat_ref[c * mm_rows:(c + 1) * mm_rows, :]
                next_ref[0, c * mm_rows:(c + 1) * mm_rows, :] = jnp.dot(xc, wg, preferred_element_type=F32)
                next_ref[1, c * mm_rows:(c + 1) * mm_rows, :] = jnp.dot(xc, wv, preferred_element_type=F32)
            if c < n_ep:
                lo = c * ep_rows
                gate = conv(prev_ref, 0, lo, cwg_ref, cbg_ref)
                val = conv(prev_ref, 1, lo, cwv_ref, cbv_ref)
                o_ref[lo:lo + ep_rows, :] = (jax.nn.silu(gate) * val).astype(o_ref.dtype)

    @pl.when(j % 2 == 0)
    def _():
        stage(ub_ref, ua_ref)

    @pl.when(j % 2 == 1)
    def _():
        stage(ua_ref, ub_ref)


def _ffn_up(xn, w_up, conv_w, conv_b, *, layer, tm, tn, name):
    rows, d = xn.shape
    d_ff = w_up.shape[2] // 2
    nj = d_ff // tn
    hb = tm // HALO_BF16
    cb = conv_b.reshape(conv_b.shape[0], 1, 2 * d_ff)
    cur = lambda j: jnp.minimum(j, nj - 1)
    prev = lambda j: jnp.maximum(j - 1, 0)
    sq = pl.Squeezed()
    return pl.pallas_call(
        functools.partial(_ffn_up_body, n_mm=_nchunks(tm + HALO_BF16), n_ep=_nchunks(tm)),
        grid=(rows // tm, nj + 1),
        in_specs=[pl.BlockSpec((tm, d), lambda i, j: (i, 0)),
                  pl.BlockSpec((HALO_BF16, d), lambda i, j: (jnp.maximum(i * hb - 1, 0), 0)),
                  pl.BlockSpec((sq, d, tn), lambda i, j: (layer, 0, cur(j))),
                  pl.BlockSpec((sq, d, tn), lambda i, j: (layer, 0, nj + cur(j))),
                  pl.BlockSpec((sq, FFN_CONV, tn), lambda i, j: (layer, 0, prev(j))),
                  pl.BlockSpec((sq, FFN_CONV, tn), lambda i, j: (layer, 0, nj + prev(j))),
                  pl.BlockSpec((sq, 1, tn), lambda i, j: (layer, 0, prev(j))),
                  pl.BlockSpec((sq, 1, tn), lambda i, j: (layer, 0, nj + prev(j)))],
        out_specs=pl.BlockSpec((tm, tn), lambda i, j: (i, prev(j))),
        out_shape=jax.ShapeDtypeStruct((rows, d_ff), BF16),
        scratch_shapes=[pltpu.VMEM((tm + HALO_BF16, d), BF16),
                        pltpu.VMEM((2, tm + HALO_BF16, tn), F32),
                        pltpu.VMEM((2, tm + HALO_BF16, tn), F32)],
        compiler_params=_cp(("parallel", "arbitrary"), 56),
        name=name,
    )(xn, xn, w_up, w_up, conv_w, conv_w, cb, cb)


def _cumsum_chunks(x):
    row = lax.broadcasted_iota(jnp.int32, x.shape, 0) % SCAN_T
    sh = 1
    while sh < SCAN_T:
        x = x + jnp.where(row >= sh, pltpu.roll(x, sh, 0), 0.0)
        sh *= 2
    return x


def _chunk_rows(rows, reps):
    return jnp.concatenate([jnp.broadcast_to(r, (reps, r.shape[1])) for r in rows], axis=0)


def _hgrn2_body(q_ref, f_ref, v_ref, g_ref, lb_ref, hg_ref, o_ref, st_ref, *, nsub):
    @pl.when(pl.program_id(1) == 0)
    def _():
        st_ref[...] = jnp.zeros_like(st_ref)

    t = SCAN_T
    nblk = t // SUB_T
    lb = lb_ref[...]
    ri = lax.broadcasted_iota(jnp.int32, (t, t), 0)
    ci = lax.broadcasted_iota(jnp.int32, (t, t), 1)
    tril = ci <= ri

    q = q_ref[...]
    fp = f_ref[...]
    vb = v_ref[...].astype(BF16)
    row = lax.broadcasted_iota(jnp.int32, q.shape, 0) % t
    lf = jnp.log(lb + (1.0 - lb) * jax.nn.sigmoid(fp))
    k = (1.0 - lb) * jax.nn.sigmoid(-fp)
    b = _cumsum_chunks(lf)
    ref = [[b[c * t + u * SUB_T + SUB_T // 2 - 1:c * t + u * SUB_T + SUB_T // 2, :] for u in range(nblk)]
           for c in range(nsub)]
    last = [b[c * t + t - 1:c * t + t, :] for c in range(nsub)]
    r_own = _chunk_rows([ref[c][u] for c in range(nsub) for u in range(nblk)], SUB_T)
    qt = (q * jnp.exp(b - r_own)).astype(BF16)
    kh = [jnp.where(row < (u + 1) * SUB_T,
                    k * jnp.exp(_chunk_rows([ref[c][u] for c in range(nsub)], t) - b), 0.0).astype(BF16)
          for u in range(nblk)]
    kd = (k * jnp.exp(_chunk_rows(last, t) - b)).astype(BF16)
    qe = (q * jnp.exp(b)).astype(BF16)

    intra, upd = [], []
    for c in range(nsub):
        lo = c * t
        rows = [lax.dot_general(qt[lo + u * SUB_T:lo + (u + 1) * SUB_T, :], kh[u][lo:lo + t, :], NT_DIMS,
                                preferred_element_type=F32) for u in range(nblk)]
        a = jnp.where(tril, jnp.concatenate(rows, axis=0), 0.0).astype(BF16)
        intra.append(jnp.dot(a, vb[lo:lo + t, :], preferred_element_type=F32))
        upd.append(lax.dot_general(vb[lo:lo + t, :], kd[lo:lo + t, :], TN_DIMS, preferred_element_type=F32))

    st = st_ref[...]
    outs = []
    for c in range(nsub):
        outs.append(intra[c] + lax.dot_general(qe[c * t:(c + 1) * t, :], st.astype(BF16), NT_DIMS,
                                               preferred_element_type=F32))
        st = st * jnp.exp(last[c]) + upd[c]
    st_ref[...] = st
    o = _rms(jnp.concatenate(outs, axis=0), hg_ref[...])
    o_ref[...] = (o * jax.nn.silu(g_ref[...])).astype(o_ref.dtype)


def _hgrn2(proj, lb, hg, *, heads, tb, name):
    rows = proj.shape[0]
    blk = lambda off: pl.BlockSpec((tb, A_DK), lambda h, c, off=off: (c, off + h))
    vec = pl.BlockSpec((pl.Squeezed(), 1, A_DK), lambda h, c: (h, 0, 0))
    return pl.pallas_call(
        functools.partial(_hgrn2_body, nsub=tb // SCAN_T),
        grid=(heads, rows // tb),
        in_specs=[blk(0), blk(heads), blk(2 * heads), blk(3 * heads), vec, vec],
        out_specs=pl.BlockSpec((tb, A_DV), lambda h, c: (c, h)),
        out_shape=jax.ShapeDtypeStruct((rows, heads * A_DV), BF16),
        scratch_shapes=[pltpu.VMEM((A_DV, A_DK), F32)],
        compiler_params=_cp(("parallel", "arbitrary")),
        name=name,
    )(proj, proj, proj, proj, lb.reshape(heads, 1, A_DK), hg.reshape(heads, 1, A_DV))


def _kprep_body(kr_ref, cs_ref, o_ref):
    t = kr_ref[...] * cs_ref[...]
    o_ref[...] = (t + pltpu.roll(t, B_ROPE, 1)).astype(o_ref.dtype)


def _kprep(krp, cs, *, tm, name):
    rows = krp.shape[0]
    spec = pl.BlockSpec((tm, LANES), lambda i: (i, 0))
    return pl.pallas_call(
        _kprep_body, grid=(rows // tm,), in_specs=[spec, spec], out_specs=spec,
        out_shape=jax.ShapeDtypeStruct((rows, LANES), BF16),
        compiler_params=_cp(("parallel",)), name=name,
    )(krp, cs)


def _attn_body(qn_ref, qp_ref, cs_ref, kn_ref, kp_ref, vt_ref, o_ref, kcat_ref, *, tq, tk, n_kv, scale, hpb):
    i = pl.program_id(1)

    hs = [slice(h * LANES, (h + 1) * LANES) for h in range(hpb)]

    @pl.when(i == 0)
    def _():
        for h in range(hpb):
            kcat_ref[h, :, 0:B_NOPE] = kn_ref[:, hs[h]]
            kcat_ref[h, :, B_NOPE:] = kp_ref[...]

    qs = i * tq
    cs = cs_ref[...]
    qcat = [jnp.concatenate([(qn_ref[:, hs[h]] * scale).astype(BF16),
                             (qp_ref[:, hs[h]] * cs * scale).astype(BF16)], axis=1) for h in range(hpb)]
    off = CHUNK - N_META
    first_cid = (qs + off) // CHUNK
    last_cid = (qs + tq - 1 + off) // CHUNK
    n_full = jnp.minimum((CHUNK * first_cid + N_META) // tk, n_kv)
    n_vis = jnp.minimum((CHUNK * last_cid + N_META - 1) // tk + 1, n_kv)
    q_cid = (qs + lax.broadcasted_iota(jnp.int32, (tk, tq), 1) + off) // CHUNK
    k_iota = lax.broadcasted_iota(jnp.int32, (tk, tq), 0) + off

    def make_step(masked):
        def step(j, carry):
            ks = pl.multiple_of(j * tk, tk)
            s = [lax.dot_general(kcat_ref[h, pl.ds(ks, tk), :], qcat[h], NT_DIMS, preferred_element_type=F32)
                 for h in range(hpb)]
            out = []
            for h in range(hpb):
                m, l, acc = carry[h]
                sh = jnp.where((k_iota + ks) // CHUNK <= q_cid, s[h], -jnp.inf) if masked else s[h]
                m_new = jnp.maximum(m, jnp.max(sh, axis=0, keepdims=True))
                p = jnp.exp2(sh - m_new)
                alpha = jnp.exp2(m - m_new)
                l = alpha * l + jnp.sum(p, axis=0, keepdims=True)
                acc = alpha * acc + jnp.dot(vt_ref[hs[h], pl.ds(ks, tk)], p.astype(BF16),
                                            preferred_element_type=F32)
                out.append((m_new, l, acc))
            return tuple(out)
        return step

    init = tuple((jnp.full((1, tq), -jnp.inf, F32), jnp.zeros((1, tq), F32), jnp.zeros((B_DV, tq), F32))
                 for _ in range(hpb))
    carry = lax.fori_loop(0, n_full, make_step(False), init)
    carry = lax.fori_loop(n_full, n_vis, make_step(True), carry)
    for h in range(hpb):
        _, l, acc = carry[h]
        o_ref[:, hs[h]] = (acc / l).T.astype(o_ref.dtype)


def _attention(qf, cs, kn, kp, vt, *, heads, tq, tk, name, hpb=2):
    rows = qf.shape[0]
    scale = float(B_NOPE + B_ROPE) ** -0.5 * math.log2(math.e)
    wide = hpb * LANES
    groups = heads // hpb
    return pl.pallas_call(
        functools.partial(_attn_body, tq=tq, tk=tk, n_kv=rows // tk, scale=scale, hpb=hpb),
        grid=(groups, rows // tq),
        in_specs=[pl.BlockSpec((tq, wide), lambda h, i: (i, h)),
                  pl.BlockSpec((tq, wide), lambda h, i: (i, groups + h)),
                  pl.BlockSpec((tq, LANES), lambda h, i: (i, 0)),
                  pl.BlockSpec((rows, wide), lambda h, i: (0, h)),
                  pl.BlockSpec((rows, LANES), lambda h, i: (0, 0)),
                  pl.BlockSpec((wide, rows), lambda h, i: (h, 0))],
        out_specs=pl.BlockSpec((tq, wide), lambda h, i: (i, h)),
        out_shape=jax.ShapeDtypeStruct((rows, heads * B_DV), BF16),
        scratch_shapes=[pltpu.VMEM((hpb, rows, B_NOPE + LANES), BF16)],
        compiler_params=_cp(("parallel", "arbitrary")),
        name=name,
    )(qf, qf, cs, kn, kp, vt)


def _pool_body(u_ref, halo_ref, w_ref, sc_ref, o_ref, *, tm, gdim):
    i = pl.program_id(0)
    hl = halo_ref[...]
    hl = jnp.where(i == 0, jnp.zeros_like(hl), hl)
    halo = hl.shape[0]
    pos = i * tm + lax.broadcasted_iota(jnp.int32, (tm, gdim), 0)
    for gi, win in enumerate(POOL_WINDOWS):
        cols = slice(gi * gdim, (gi + 1) * gdim)
        x = u_ref[:, cols]
        s = jnp.concatenate([hl[:, cols], x], axis=0)
        sh = 1
        while sh < win:
            s = s + pltpu.roll(s, sh, 0)
            sh *= 2
        cnt = jnp.minimum(pos + 1, win).astype(F32)
        pooled = s[halo:, :] / cnt - x
        y = jnp.dot(pooled.astype(BF16), w_ref[gi], preferred_element_type=F32)
        o_ref[:, cols] = (y * sc_ref[:, cols]).astype(o_ref.dtype)


def _pool(u, pool_w, pool_scale, *, tm, name):
    rows = u.shape[0]
    groups, gdim, _ = pool_w.shape
    cw = groups * gdim
    halo = 2 * HALO_F32
    hb = tm // halo
    return pl.pallas_call(
        functools.partial(_pool_body, tm=tm, gdim=gdim),
        grid=(rows // tm,),
        in_specs=[pl.BlockSpec((tm, cw), lambda i: (i, 0)),
                  pl.BlockSpec((halo, cw), lambda i: (jnp.maximum(i * hb - 1, 0), 0)),
                  pl.BlockSpec((groups, gdim, gdim), lambda i: (0, 0, 0)),
                  pl.BlockSpec((1, cw), lambda i: (0, 0))],
        out_specs=pl.BlockSpec((tm, cw), lambda i: (i, 0)),
        out_shape=jax.ShapeDtypeStruct((rows, cw), BF16),
        compiler_params=_cp(("parallel",)),
        name=name,
    )(u, u, pool_w.astype(BF16), pool_scale.reshape(1, cw))


def _split3_dot(x, sel):
    hi = x.astype(BF16)
    r1 = x - hi.astype(F32)
    mid = r1.astype(BF16)
    lo = (r1 - mid.astype(F32)).astype(BF16)
    return (jnp.dot(hi, sel, preferred_element_type=F32) + jnp.dot(mid, sel, preferred_element_type=F32)
            + jnp.dot(lo, sel, preferred_element_type=F32))


def _ssd_body(xs_ref, xh_ref, b_ref, bh_ref, c_ref, ch_ref, dt_ref, z_ref, cwx_ref, cwb_ref, cwc_ref,
              cbx_ref, cbb_ref, cbc_ref, dtb_ref, ah_ref, dsk_ref, o_ref, st_ref, *, nsub, hpg):
    g = pl.program_id(0)
    first = pl.program_id(1) == 0

    @pl.when(first)
    def _():
        st_ref[...] = jnp.zeros_like(st_ref)

    t = SCAN_T
    gw = hpg * D_HEAD_DIM

    def conv(x_ref, halo_ref, cw_ref, cb_ref):
        hl = halo_ref[...]
        x = jnp.concatenate([jnp.where(first, jnp.zeros_like(hl), hl), x_ref[...]], axis=0)
        cw = cw_ref[...]
        y = cw[D_CONV - 1:D_CONV, :] * x + cb_ref[...]
        for kk in range(1, D_CONV):
            y = y + cw[D_CONV - 1 - kk:D_CONV - kk, :] * pltpu.roll(x, kk, 0)
        return jax.nn.silu(y[HALO_F32:, :])

    xs = conv(xs_ref, xh_ref, cwx_ref, cbx_ref)
    bb = conv(b_ref, bh_ref, cwb_ref, cbb_ref).astype(BF16)
    cc = conv(c_ref, ch_ref, cwc_ref, cbc_ref).astype(BF16)
    dt = jax.nn.softplus(dt_ref[...] + dtb_ref[...])
    acs_h = _cumsum_chunks(dt * ah_ref[...])
    hrow = lax.broadcasted_iota(jnp.int32, (LANES, gw), 0)
    hcol = lax.broadcasted_iota(jnp.int32, (LANES, gw), 1) // D_HEAD_DIM + g * hpg
    sel = (hrow == hcol).astype(BF16)
    dtx = _split3_dot(dt, sel)
    acs = _split3_dot(acs_h, sel)
    xd = xs * dtx
    decay_in = jnp.exp(acs)
    last = [acs[c * t + t - 1:c * t + t, :] for c in range(nsub)]
    xdd = (xd * jnp.exp(_chunk_rows(last, t) - acs)).astype(BF16)

    pr = lax.broadcasted_iota(jnp.int32, (t, 2 * t), 0)
    pc = lax.broadcasted_iota(jnp.int32, (t, 2 * t), 1)
    pj = pc % t
    tril2 = pj <= pr
    eye2f = (pr == pj).astype(F32)
    left = pc < t

    diag, upd = [], []
    for c in range(nsub):
        rs = slice(c * t, (c + 1) * t)
        bc = bb[rs, :]
        cb2 = lax.dot_general(cc[rs, :], jnp.concatenate([bc, bc], axis=0), NT_DIMS, preferred_element_type=F32)
        parts = []
        for p in range(hpg // 2):
            cols = slice(p * 2 * D_HEAD_DIM, (p + 1) * 2 * D_HEAD_DIM)
            ac = acs[rs, cols]
            row_cs = jnp.sum(ac * eye2f, axis=0, keepdims=True)
            lmat = jnp.where(tril2, jnp.exp(ac - row_cs), 0.0)
            xdp = xd[rs, cols]
            rhs = jnp.concatenate([jnp.where(left, xdp, 0.0), jnp.where(left, 0.0, xdp)], axis=0)
            parts.append(jnp.dot((cb2 * lmat).astype(BF16), rhs.astype(BF16), preferred_element_type=F32))
        diag.append(jnp.concatenate(parts, axis=1))
        upd.append(lax.dot_general(bc, xdd[rs, :], TN_DIMS, preferred_element_type=F32))

    st = st_ref[...]
    ys = []
    for c in range(nsub):
        rs = slice(c * t, (c + 1) * t)
        ys.append(diag[c] + decay_in[rs, :] * jnp.dot(cc[rs, :], st.astype(BF16), preferred_element_type=F32))
        st = st * jnp.exp(last[c]) + upd[c]
    st_ref[...] = st
    y = jnp.concatenate(ys, axis=0) + xs * dsk_ref[...]
    o_ref[...] = y * jax.nn.silu(z_ref[...])


def _ssd(xbc, dt, z, conv_w, conv_b, dt_bias, a_h, d_ch, *, heads, tb, name):
    rows, ch = xbc.shape
    inner = heads * D_HEAD_DIM
    hpg = heads // D_GROUPS
    gw = hpg * D_HEAD_DIM
    assert hpg % 2 == 0 and 2 * D_HEAD_DIM == LANES and SCAN_T == D_HEAD_DIM
    nxb = inner // LANES
    hb = tb // HALO_F32
    prev = lambda c: jnp.maximum(c * hb - 1, 0)
    col_x = lambda g: g
    col_b = lambda g: nxb + g
    col_c = lambda g: nxb + D_GROUPS + g

    def data(width, col):
        return [pl.BlockSpec((tb, width), lambda g, c: (c, col(g))),
                pl.BlockSpec((HALO_F32, width), lambda g, c: (prev(c), col(g)))]

    def per_col(nrows, width, col):
        return pl.BlockSpec((nrows, width), lambda g, c: (0, col(g)))

    cb = conv_b.reshape(1, ch)
    return pl.pallas_call(
        functools.partial(_ssd_body, nsub=tb // SCAN_T, hpg=hpg),
        grid=(D_GROUPS, rows // tb),
        in_specs=(data(gw, col_x) + data(D_STATE, col_b) + data(D_STATE, col_c)
                  + [pl.BlockSpec((tb, LANES), lambda g, c: (c, 0)),
                     pl.BlockSpec((tb, gw), lambda g, c: (c, g)),
                     per_col(D_CONV, gw, col_x), per_col(D_CONV, D_STATE, col_b), per_col(D_CONV, D_STATE, col_c),
                     per_col(1, gw, col_x), per_col(1, D_STATE, col_b), per_col(1, D_STATE, col_c),
                     pl.BlockSpec((1, LANES), lambda g, c: (0, 0)),
                     pl.BlockSpec((1, LANES), lambda g, c: (0, 0)),
                     per_col(1, gw, col_x)]),
        out_specs=pl.BlockSpec((tb, gw), lambda g, c: (c, g)),
        out_shape=jax.ShapeDtypeStruct((rows, inner), F32),
        scratch_shapes=[pltpu.VMEM((D_STATE, gw), F32)],
        compiler_params=_cp(("parallel", "arbitrary")),
        name=name,
    )(xbc, xbc, xbc, xbc, xbc, xbc, dt, z, conv_w, conv_w, conv_w, cb, cb, cb, dt_bias, a_h, d_ch)


def _with_rot_half(w):
    half = B_ROPE // 2
    return jnp.concatenate([w, -w[..., half:], w[..., :half]], axis=-1)


def _layout_w_uq(w, heads):
    k = w.shape[0]
    w3 = w.astype(BF16).reshape(k, heads, B_NOPE + B_ROPE)
    nope = w3[:, :, :B_NOPE].reshape(k, heads * B_NOPE)
    pe = _with_rot_half(w3[:, :, B_NOPE:]).reshape(k, heads * LANES)
    return jnp.concatenate([nope, pe], axis=1)[None]


def kernel(x, meta_tokens, lb_logits, norm_g, ab_w_in, hgrn_norm_g, mla_q_norm_g, mla_kv_norm_g,
           mla_w_uq, mla_w_ukv, ab_w_out, cd_w_in, pool_w, pool_scale, ssm_conv_w, ssm_conv_b,
           ssm_dt_bias, ssm_a_log, ssm_d, ssm_norm_g, cd_w_out, ffn_w_up, ffn_conv_w, ffn_conv_b,
           ffn_w_down):
    bsz, seq, d = x.shape
    assert bsz == 1
    depth = norm_g.shape[0]
    total = N_META + seq
    lp = -(-total // Q_BLOCK) * Q_BLOCK
    a_heads = hgrn_norm_g.shape[1]
    a_kw = a_heads * A_DK
    a_w = a_heads * A_DV
    b_heads = mla_w_uq.shape[2] // (B_NOPE + B_ROPE)
    c_width = pool_scale.shape[1]
    d_inner = ssm_norm_g.shape[1]
    d_heads = ssm_dt_bias.shape[1]
    d_xbc = ssm_conv_w.shape[2]
    d_ff = ffn_w_down.shape[1]

    tm_mm = _pick(lp, (1040, 640, 512, 256, 128))
    tm_mid = _pick(lp, (640, 512, 256, 128))
    tm_row = _pick(lp, (320, 256, 128))
    tb = _pick(lp, (320, 256, 128, 64))
    tq = _pick(lp, (640, 512, 256, 128))
    tn_ff = _pick(d_ff, (256, 128))

    h = jnp.concatenate([meta_tokens.astype(x.dtype), x[0],
                         jnp.zeros((lp - total, d), x.dtype)], axis=0)

    inv = ROPE_THETA ** (-jnp.arange(0, B_ROPE, 2, dtype=F32) / B_ROPE)
    ang = jnp.arange(lp, dtype=F32)[:, None] * inv[None, :]
    cos, sin = jnp.cos(ang), jnp.sin(ang)
    cs = jnp.concatenate([cos, cos, sin, sin], axis=-1)
    lb_all = jnp.cumsum(jax.nn.softmax(lb_logits.astype(F32), axis=0), axis=0)
    w_up = ffn_w_up.astype(BF16)
    w_down = ffn_w_down.astype(BF16)

    xn = _rownorm(h, norm_g[0, 0], width=d, tm=tm_row, name="norm_in")

    for layer in range(depth):
        g = norm_g[layer]
        j = layer // 2
        xp = [(xn, d, 0, 0)]
        if layer % 2 == 0:
            n_ab = 2 * a_kw + 2 * a_w
            n_lora = B_Q_LORA + B_KV_LORA
            proj_a = _matmul(xp, ab_w_in, layer=j, col0=0, ncols=n_ab, tm=tm_mm, tn=512, out_dtype=F32,
                             name="ab_in_a")
            proj_c = _matmul(xp, ab_w_in, layer=j, col0=n_ab, ncols=n_lora, tm=tm_mm, tn=512, out_dtype=F32,
                             name="ab_in_c")
            w_kr = _with_rot_half(ab_w_in[j, :, n_ab + n_lora:].astype(BF16))[None]
            krp = _matmul(xp, w_kr, ncols=LANES, tm=tm_mm, tn=LANES, out_dtype=F32, name="ab_in_kr")
            o_a = _hgrn2(proj_a, lb_all[layer], hgrn_norm_g[j], heads=a_heads, tb=tb, name="hgrn2")
            cq = _rownorm(proj_c, mla_q_norm_g[j], width=B_Q_LORA, xblk=0, tm=tm_row, name="q_norm")
            ckv = _rownorm(proj_c, mla_kv_norm_g[j], width=B_KV_LORA, xblk=B_Q_LORA // B_KV_LORA, tm=tm_row,
                           name="kv_norm")
            w_uq = _layout_w_uq(mla_w_uq[j], b_heads)
            qf = _matmul([(cq, B_Q_LORA, 0, 0)], w_uq, ncols=w_uq.shape[2], tm=tm_mm, tn=512, out_dtype=F32,
                         name="uq")
            w_kv = mla_w_ukv[j].astype(BF16).reshape(B_KV_LORA, b_heads, 2, B_NOPE)
            w_k = w_kv[:, :, 0, :].reshape(1, B_KV_LORA, b_heads * B_NOPE)
            w_vt = w_kv[:, :, 1, :].reshape(B_KV_LORA, b_heads * B_DV).T
            kn = _matmul([(ckv, B_KV_LORA, 0, 0)], w_k, ncols=b_heads * B_NOPE, tm=tm_mm, tn=512,
                         out_dtype=BF16, name="uk")
            vt = _matmul_nt(w_vt, ckv, tm=tm_mid, tn=512, name="uv_t")
            kp = _kprep(krp, cs, tm=tm_row, name="k_rope")
            o_b = _attention(qf, cs, kn, kp, vt, heads=b_heads, tq=tq, tk=tq, name="attention")
            half = a_w
            mix = [(o_a, half, 0, 0), (o_b, half, 0, 1)]
            w_out, w_out_layer = ab_w_out, j
        else:
            n_zu = d_inner + c_width
            u_c = _matmul(xp, cd_w_in, layer=j, col0=0, ncols=c_width, tm=tm_mm, tn=512, out_dtype=F32,
                          name="cd_in_u")
            z = _matmul(xp, cd_w_in, layer=j, col0=c_width, ncols=d_inner, tm=tm_mm, tn=512, out_dtype=F32,
                        name="cd_in_z")
            xbc = _matmul(xp, cd_w_in, layer=j, col0=n_zu, ncols=d_xbc, tm=tm_mm, tn=512, out_dtype=F32,
                          name="cd_in_xbc")
            dt_pad = LANES - d_heads
            w_dt = jnp.pad(cd_w_in[j, :, n_zu + d_xbc:].astype(BF16), ((0, 0), (0, dt_pad)))[None]
            dt = _matmul(xp, w_dt, ncols=LANES, tm=tm_mm, tn=LANES, out_dtype=F32, name="cd_in_dt")
            o_c = _pool(u_c, pool_w[j], pool_scale[j], tm=tm_row, name="pool")
            a_h = jnp.pad(-jnp.exp(ssm_a_log[j].astype(F32)), (0, dt_pad)).reshape(1, LANES)
            d_ch = jnp.repeat(ssm_d[j].astype(F32), D_HEAD_DIM).reshape(1, d_inner)
            dtb = jnp.pad(ssm_dt_bias[j].astype(F32), (0, dt_pad)).reshape(1, LANES)
            y = _ssd(xbc, dt, z, ssm_conv_w[j], ssm_conv_b[j], dtb, a_h, d_ch, heads=d_heads, tb=tb, name="ssd")
            yn = _rownorm(y, ssm_norm_g[j], width=d_inner, tm=tm_row, name="ssd_norm")
            mix = [(o_c, c_width, 0, 0)] + [(yn, c_width, b, b + 1) for b in range(d_inner // c_width)]
            w_out, w_out_layer = cd_w_out, j

        mixed = _matmul(mix, w_out, layer=w_out_layer, ncols=d, tm=tm_mm, tn=512, out_dtype=F32, name="mix_out")
        h, xn = _addnorm(h, mixed, g[1], g[2], tm=tm_row, name="add_norm_mix")
        act = _ffn_up(xn, w_up, ffn_conv_w, ffn_conv_b, layer=layer, tm=tm_mm, tn=tn_ff, name="ffn_up")
        ff = _matmul([(act, d_ff, 0, 0)], w_down, layer=layer, ncols=d, tm=tm_mid, tn=256, out_dtype=F32,
                     name="ffn_down")
        if layer + 1 < depth:
            h, xn = _addnorm(h, ff, g[3], norm_g[layer + 1, 0], tm=tm_row, name="add_norm_ffn")

    out = _add_rows(h, ff, norm_g[depth - 1, 3], row0=N_META, nrows=seq, tm=_pick(seq, (256, 240, 128, 16, 8)),
                    name="add_out")
    return out[None]
```

```python
import functools
import math

import jax
import jax.numpy as jnp
from jax import lax
from jax.experimental import pallas as pl
from jax.experimental.pallas import tpu as pltpu

F32 = jnp.float32
BF16 = jnp.bfloat16
HIGHEST = lax.Precision.HIGHEST

CHUNK = 64
N_META = 16
Q_BLOCK = 128
NORM_EPS = 1e-6
A_DK = 128
A_DV = 128
B_NOPE = 128
B_ROPE = 64
B_DV = 128
B_Q_LORA = 1024
B_KV_LORA = 512
ROPE_THETA = 10000.0
POOL_WINDOWS = (2, 4, 8, 16)
D_HEAD_DIM = 64
D_GROUPS = 8
D_STATE = 128
D_CONV = 4
FFN_CONV = 3

SCAN_T = 64
SUB_T = 16
LANES = 128
HALO_BF16 = 16
HALO_F32 = 8
MIB = 1024 * 1024

NT_DIMS = (((1,), (1,)), ((), ()))
TN_DIMS = (((0,), (0,)), ((), ()))


def _cp(sem, vmem_mib=48):
    return pltpu.CompilerParams(dimension_semantics=sem, vmem_limit_bytes=vmem_mib * MIB)


def _pick(n, cands):
    for c in cands:
        if n % c == 0:
            return c
    raise ValueError(f"no tile for {n} in {cands}")


def _nchunks(rows, most=6):
    tiles = rows // HALO_BF16
    return max(c for c in range(1, most + 1) if tiles % c == 0)


def _rms(x, g):
    return x * lax.rsqrt(jnp.mean(x * x, axis=-1, keepdims=True) + NORM_EPS) * g


def _mm_body(*refs, n_lhs):
    o_ref = refs[-1]
    acc = None
    for x_ref, w_ref in zip(refs[:n_lhs], refs[n_lhs:2 * n_lhs]):
        part = jnp.dot(x_ref[...], w_ref[...].astype(BF16), preferred_element_type=F32)
        acc = part if acc is None else acc + part
    o_ref[...] = acc.astype(o_ref.dtype)


def _matmul(pieces, w, *, layer=0, col0=0, ncols, tm, tn, out_dtype, name, vmem_mib=56):
    rows = pieces[0][0].shape[0]
    assert col0 % tn == 0 and ncols % tn == 0 and rows % tm == 0
    cb = col0 // tn
    x_specs = [pl.BlockSpec((tm, k), lambda i, j, xb=xb: (i, xb)) for _, k, xb, _ in pieces]
    w_specs = [pl.BlockSpec((pl.Squeezed(), k, tn), lambda i, j, wb=wb: (layer, wb, cb + j))
               for _, k, _, wb in pieces]
    return pl.pallas_call(
        functools.partial(_mm_body, n_lhs=len(pieces)),
        grid=(rows // tm, ncols // tn),
        in_specs=x_specs + w_specs,
        out_specs=pl.BlockSpec((tm, tn), lambda i, j: (i, j)),
        out_shape=jax.ShapeDtypeStruct((rows, ncols), out_dtype),
        compiler_params=_cp(("parallel", "parallel"), vmem_mib),
        name=name,
    )(*[p[0] for p in pieces], *([w] * len(pieces)))


def _take_cols_body(w_ref, o_ref, *, c0):
    o_ref[...] = w_ref[:, c0:c0 + o_ref.shape[1]]


def _take_cols(w, *, layer, c0, width, tr, name):
    _, k, n = w.shape
    assert c0 % LANES == 0 and k % tr == 0
    return pl.pallas_call(
        functools.partial(_take_cols_body, c0=c0),
        grid=(k // tr,),
        in_specs=[pl.BlockSpec((pl.Squeezed(), tr, n), lambda i: (layer, i, 0))],
        out_specs=pl.BlockSpec((tr, width), lambda i: (i, 0)),
        out_shape=jax.ShapeDtypeStruct((k, width), w.dtype),
        compiler_params=_cp(("parallel",)),
        name=name,
    )(w)


def _mm_nt_body(w_ref, x_ref, o_ref):
    o_ref[...] = lax.dot_general(w_ref[...], x_ref[...], NT_DIMS,
                                 preferred_element_type=F32).astype(o_ref.dtype)


def _matmul_nt(wt, x, *, tm, tn, name):
    n, k = wt.shape
    rows = x.shape[0]
    return pl.pallas_call(
        _mm_nt_body,
        grid=(n // tn, rows // tm),
        in_specs=[pl.BlockSpec((tn, k), lambda a, b: (a, 0)),
                  pl.BlockSpec((tm, k), lambda a, b: (b, 0))],
        out_specs=pl.BlockSpec((tn, tm), lambda a, b: (a, b)),
        out_shape=jax.ShapeDtypeStruct((n, rows), BF16),
        compiler_params=_cp(("parallel", "parallel")),
        name=name,
    )(wt, x)


def _rownorm_body(x_ref, g_ref, o_ref):
    o_ref[...] = _rms(x_ref[...].astype(F32), g_ref[...]).astype(o_ref.dtype)


def _rownorm(x, g, *, width, xblk=0, tm, name):
    rows = x.shape[0]
    return pl.pallas_call(
        _rownorm_body,
        grid=(rows // tm,),
        in_specs=[pl.BlockSpec((tm, width), lambda i: (i, xblk)),
                  pl.BlockSpec((1, width), lambda i: (0, 0))],
        out_specs=pl.BlockSpec((tm, width), lambda i: (i, 0)),
        out_shape=jax.ShapeDtypeStruct((rows, width), BF16),
        compiler_params=_cp(("parallel",)),
        name=name,
    )(x, g.reshape(1, width).astype(F32))


def _addnorm_body(h_ref, m_ref, g1_ref, g2_ref, ho_ref, xo_ref):
    hn = h_ref[...] + _rms(m_ref[...], g1_ref[...])
    ho_ref[...] = hn
    xo_ref[...] = _rms(hn, g2_ref[...]).astype(xo_ref.dtype)


def _add_body(h_ref, m_ref, g1_ref, ho_ref):
    ho_ref[...] = h_ref[...] + _rms(m_ref[...], g1_ref[...])


def _add_rows(h, m, g_post, *, row0, nrows, tm, name):
    d = h.shape[1]
    assert row0 % HALO_F32 == 0 and tm % HALO_F32 == 0
    win = pl.BlockSpec((pl.Element(tm), pl.Element(d)),
                       lambda i: (pl.multiple_of(row0 + i * tm, HALO_F32), 0))
    return pl.pallas_call(
        _add_body, grid=(nrows // tm,),
        in_specs=[win, win, pl.BlockSpec((1, d), lambda i: (0, 0))],
        out_specs=pl.BlockSpec((tm, d), lambda i: (i, 0)),
        out_shape=jax.ShapeDtypeStruct((nrows, d), F32),
        compiler_params=_cp(("parallel",)), name=name,
    )(h, m, g_post.reshape(1, d))


def _addnorm(h, m, g_post, g_next, *, tm, name):
    rows, d = h.shape
    row_spec = pl.BlockSpec((tm, d), lambda i: (i, 0))
    g_spec = pl.BlockSpec((1, d), lambda i: (0, 0))
    return pl.pallas_call(
        _addnorm_body, grid=(rows // tm,),
        in_specs=[row_spec, row_spec, g_spec, g_spec], out_specs=[row_spec, row_spec],
        out_shape=[jax.ShapeDtypeStruct((rows, d), F32), jax.ShapeDtypeStruct((rows, d), BF16)],
        compiler_params=_cp(("parallel",)), name=name,
    )(h, m, g_post.reshape(1, d), g_next.reshape(1, d))


def _ffn_up_body(x_ref, halo_ref, wg_ref, wv_ref, cwg_ref, cwv_ref, cbg_ref, cbv_ref, o_ref,
                 xcat_ref, *, n_mm):
    i = pl.program_id(0)

    @pl.when(pl.program_id(1) == 0)
    def _():
        hl = halo_ref[...]
        xcat_ref[0:HALO_BF16, :] = jnp.where(i == 0, jnp.zeros_like(hl), hl)
        xcat_ref[HALO_BF16:, :] = x_ref[...]

    mm_rows = xcat_ref.shape[0] // n_mm
    wg = wg_ref[...].astype(BF16)
    wv = wv_ref[...].astype(BF16)
    cwg, cwv = cwg_ref[...], cwv_ref[...]
    cbg, cbv = cbg_ref[...], cbv_ref[...]

    def matmuls(c):
        xc = xcat_ref[c * mm_rows:(c + 1) * mm_rows, :]
        return (jnp.dot(xc, wg, preferred_element_type=F32), jnp.dot(xc, wv, preferred_element_type=F32))

    def conv(u, above, cw, cb):
        u = jnp.concatenate([above, u], axis=0)
        y = cw[2:3, :] * u + cw[1:2, :] * pltpu.roll(u, 1, 0) + cw[0:1, :] * pltpu.roll(u, 2, 0) + cb
        return y[HALO_F32:, :]

    def epilogue(c, u, prev):
        gate = conv(u[0], prev[0], cwg, cbg)
        val = conv(u[1], prev[1], cwv, cbv)
        act = (jax.nn.silu(gate) * val).astype(o_ref.dtype)
        if c == 0:
            o_ref[0:mm_rows - HALO_BF16, :] = act[HALO_BF16:, :]
        else:
            o_ref[c * mm_rows - HALO_BF16:(c + 1) * mm_rows - HALO_BF16, :] = act

    tail = lambda u: (u[0][mm_rows - HALO_F32:, :], u[1][mm_rows - HALO_F32:, :])
    zeros = jnp.zeros((HALO_F32, wg.shape[1]), F32)
    above = (zeros, zeros)
    u_cur = matmuls(0)
    for c in range(n_mm):
        u_next = matmuls(c + 1) if c + 1 < n_mm else None
        epilogue(c, u_cur, above)
        above = tail(u_cur)
        u_cur = u_next


def _ffn_up(xn, w_up, conv_w, conv_b, *, layer, tm, tn, name):
    rows, d = xn.shape
    d_ff = w_up.shape[2] // 2
    nj = d_ff // tn
    hb = tm // HALO_BF16
    cb = conv_b.reshape(conv_b.shape[0], 1, 2 * d_ff)
    sq = pl.Squeezed()
    return pl.pallas_call(
        functools.partial(_ffn_up_body, n_mm=_nchunks(tm + HALO_BF16)),
        grid=(rows // tm, nj),
        in_specs=[pl.BlockSpec((tm, d), lambda i, j: (i, 0)),
                  pl.BlockSpec((HALO_BF16, d), lambda i, j: (jnp.maximum(i * hb - 1, 0), 0)),
                  pl.BlockSpec((sq, d, tn), lambda i, j: (layer, 0, j)),
                  pl.BlockSpec((sq, d, tn), lambda i, j: (layer, 0, nj + j)),
                  pl.BlockSpec((sq, FFN_CONV, tn), lambda i, j: (layer, 0, j)),
                  pl.BlockSpec((sq, FFN_CONV, tn), lambda i, j: (layer, 0, nj + j)),
                  pl.BlockSpec((sq, 1, tn), lambda i, j: (layer, 0, j)),
                  pl.BlockSpec((sq, 1, tn), lambda i, j: (layer, 0, nj + j))],
        out_specs=pl.BlockSpec((tm, tn), lambda i, j: (i, j)),
        out_shape=jax.ShapeDtypeStruct((rows, d_ff), BF16),
        scratch_shapes=[pltpu.VMEM((tm + HALO_BF16, d), BF16)],
        compiler_params=_cp(("parallel", "arbitrary"), 56),
        name=name,
    )(xn, xn, w_up, w_up, conv_w, conv_w, cb, cb)


def _cumsum_chunks(x):
    row = lax.broadcasted_iota(jnp.int32, x.shape, 0) % SCAN_T
    sh = 1
    while sh < SCAN_T:
        x = x + jnp.where(row >= sh, pltpu.roll(x, sh, 0), 0.0)
        sh *= 2
    return x


def _chunk_rows(rows, reps):
    return jnp.concatenate([jnp.broadcast_to(r, (reps, r.shape[1])) for r in rows], axis=0)


def _hgrn2_body(q_ref, f_ref, v_ref, g_ref, lb_ref, hg_ref, o_ref, st_ref, *, nsub):
    @pl.when(pl.program_id(1) == 0)
    def _():
        st_ref[...] = jnp.zeros_like(st_ref)

    t = SCAN_T
    nblk = t // SUB_T
    lb = lb_ref[...]
    ri = lax.broadcasted_iota(jnp.int32, (t, t), 0)
    ci = lax.broadcasted_iota(jnp.int32, (t, t), 1)
    tril = ci <= ri

    q = q_ref[...]
    fp = f_ref[...]
    vb = v_ref[...].astype(BF16)
    row = lax.broadcasted_iota(jnp.int32, q.shape, 0) % t
    lf = jnp.log(lb + (1.0 - lb) * jax.nn.sigmoid(fp))
    k = (1.0 - lb) * jax.nn.sigmoid(-fp)
    b = _cumsum_chunks(lf)
    ref = [[b[c * t + u * SUB_T + SUB_T // 2 - 1:c * t + u * SUB_T + SUB_T // 2, :] for u in range(nblk)]
           for c in range(nsub)]
    last = [b[c * t + t - 1:c * t + t, :] for c in range(nsub)]
    r_own = _chunk_rows([ref[c][u] for c in range(nsub) for u in range(nblk)], SUB_T)
    qt = (q * jnp.exp(b - r_own)).astype(BF16)
    kh = [jnp.where(row < (u + 1) * SUB_T,
                    k * jnp.exp(_chunk_rows([ref[c][u] for c in range(nsub)], t) - b), 0.0).astype(BF16)
          for u in range(nblk)]
    kd = (k * jnp.exp(_chunk_rows(last, t) - b)).astype(BF16)
    qe = (q * jnp.exp(b)).astype(BF16)

    intra, upd = [], []
    for c in range(nsub):
        lo = c * t
        rows = [lax.dot_general(qt[lo + u * SUB_T:lo + (u + 1) * SUB_T, :], kh[u][lo:lo + t, :], NT_DIMS,
                                preferred_element_type=F32) for u in range(nblk)]
        a = jnp.where(tril, jnp.concatenate(rows, axis=0), 0.0).astype(BF16)
        intra.append(jnp.dot(a, vb[lo:lo + t, :], preferred_element_type=F32))
        upd.append(lax.dot_general(vb[lo:lo + t, :], kd[lo:lo + t, :], TN_DIMS, preferred_element_type=F32))

    st = st_ref[...]
    outs = []
    for c in range(nsub):
        outs.append(intra[c] + lax.dot_general(qe[c * t:(c + 1) * t, :], st.astype(BF16), NT_DIMS,
                                               preferred_element_type=F32))
        st = st * jnp.exp(last[c]) + upd[c]
    st_ref[...] = st
    o = _rms(jnp.concatenate(outs, axis=0), hg_ref[...])
    o_ref[...] = (o * jax.nn.silu(g_ref[...])).astype(o_ref.dtype)


def _hgrn2(proj, lb, hg, *, heads, tb, name):
    rows = proj.shape[0]
    blk = lambda off: pl.BlockSpec((tb, A_DK), lambda h, c, off=off: (c, off + h))
    vec = pl.BlockSpec((pl.Squeezed(), 1, A_DK), lambda h, c: (h, 0, 0))
    return pl.pallas_call(
        functools.partial(_hgrn2_body, nsub=tb // SCAN_T),
        grid=(heads, rows // tb),
        in_specs=[blk(0), blk(heads), blk(2 * heads), blk(3 * heads), vec, vec],
        out_specs=pl.BlockSpec((tb, A_DV), lambda h, c: (c, h)),
        out_shape=jax.ShapeDtypeStruct((rows, heads * A_DV), BF16),
        scratch_shapes=[pltpu.VMEM((A_DV, A_DK), F32)],
        compiler_params=_cp(("parallel", "arbitrary")),
        name=name,
    )(proj, proj, proj, proj, lb.reshape(heads, 1, A_DK), hg.reshape(heads, 1, A_DV))


def _kprep_body(kr_ref, cs_ref, o_ref):
    t = kr_ref[...] * cs_ref[...]
    o_ref[...] = (t + pltpu.roll(t, B_ROPE, 1)).astype(o_ref.dtype)


def _kprep(krp, cs, *, tm, name):
    rows = krp.shape[0]
    spec = pl.BlockSpec((tm, LANES), lambda i: (i, 0))
    return pl.pallas_call(
        _kprep_body, grid=(rows // tm,), in_specs=[spec, spec], out_specs=spec,
        out_shape=jax.ShapeDtypeStruct((rows, LANES), BF16),
        compiler_params=_cp(("parallel",)), name=name,
    )(krp, cs)


def _attn_body(qn_ref, qp_ref, cs_ref, kn_ref, kp_ref, vt_ref, o_ref, kcat_ref, *, tq, tk, n_kv, scale, hpb):
    i = pl.program_id(1)

    hs = [slice(h * LANES, (h + 1) * LANES) for h in range(hpb)]

    @pl.when(i == 0)
    def _():
        for h in range(hpb):
            kcat_ref[h, :, 0:B_NOPE] = kn_ref[:, hs[h]]
            kcat_ref[h, :, B_NOPE:] = kp_ref[...]

    qs = i * tq
    cs = cs_ref[...]
    qcat = [jnp.concatenate([(qn_ref[:, hs[h]] * scale).astype(BF16),
                             (qp_ref[:, hs[h]] * cs * scale).astype(BF16)], axis=1) for h in range(hpb)]
    off = CHUNK - N_META
    first_cid = (qs + off) // CHUNK
    last_cid = (qs + tq - 1 + off) // CHUNK
    n_full = jnp.minimum((CHUNK * first_cid + N_META) // tk, n_kv)
    n_vis = jnp.minimum((CHUNK * last_cid + N_META - 1) // tk + 1, n_kv)
    q_cid = (qs + lax.broadcasted_iota(jnp.int32, (tk, tq), 1) + off) // CHUNK
    k_iota = lax.broadcasted_iota(jnp.int32, (tk, tq), 0) + off

    def make_step(masked):
        def step(j, carry):
            ks = pl.multiple_of(j * tk, tk)
            s = [lax.dot_general(kcat_ref[h, pl.ds(ks, tk), :], qcat[h], NT_DIMS, preferred_element_type=F32)
                 for h in range(hpb)]
            out = []
            for h in range(hpb):
                m, l, acc = carry[h]
                sh = jnp.where((k_iota + ks) // CHUNK <= q_cid, s[h], -jnp.inf) if masked else s[h]
                m_new = jnp.maximum(m, jnp.max(sh, axis=0, keepdims=True))
                p = jnp.exp2(sh - m_new)
                alpha = jnp.exp2(m - m_new)
                l = alpha * l + jnp.sum(p, axis=0, keepdims=True)
                acc = alpha * acc + jnp.dot(vt_ref[hs[h], pl.ds(ks, tk)], p.astype(BF16),
                                            preferred_element_type=F32)
                out.append((m_new, l, acc))
            return tuple(out)
        return step

    init = tuple((jnp.full((1, tq), -jnp.inf, F32), jnp.zeros((1, tq), F32), jnp.zeros((B_DV, tq), F32))
                 for _ in range(hpb))
    carry = lax.fori_loop(0, n_full, make_step(False), init)
    carry = lax.fori_loop(n_full, n_vis, make_step(True), carry)
    for h in range(hpb):
        _, l, acc = carry[h]
        o_ref[:, hs[h]] = (acc / l).T.astype(o_ref.dtype)


def _attention(qf, cs, kn, kp, vt, *, heads, tq, tk, name, hpb=2):
    rows = qf.shape[0]
    scale = float(B_NOPE + B_ROPE) ** -0.5 * math.log2(math.e)
    wide = hpb * LANES
    groups = heads // hpb
    return pl.pallas_call(
        functools.partial(_attn_body, tq=tq, tk=tk, n_kv=rows // tk, scale=scale, hpb=hpb),
        grid=(groups, rows // tq),
        in_specs=[pl.BlockSpec((tq, wide), lambda h, i: (i, h)),
                  pl.BlockSpec((tq, wide), lambda h, i: (i, groups + h)),
                  pl.BlockSpec((tq, LANES), lambda h, i: (i, 0)),
                  pl.BlockSpec((rows, wide), lambda h, i: (0, h)),
                  pl.BlockSpec((rows, LANES), lambda h, i: (0, 0)),
                  pl.BlockSpec((wide, rows), lambda h, i: (h, 0))],
        out_specs=pl.BlockSpec((tq, wide), lambda h, i: (i, h)),
        out_shape=jax.ShapeDtypeStruct((rows, heads * B_DV), BF16),
        scratch_shapes=[pltpu.VMEM((hpb, rows, B_NOPE + LANES), BF16)],
        compiler_params=_cp(("parallel", "arbitrary")),
        name=name,
    )(qf, qf, cs, kn, kp, vt)


def _pool_body(u_ref, halo_ref, w_ref, sc_ref, o_ref, *, tm, gdim):
    i = pl.program_id(0)
    hl = halo_ref[...]
    hl = jnp.where(i == 0, jnp.zeros_like(hl), hl)
    halo = hl.shape[0]
    pos = i * tm + lax.broadcasted_iota(jnp.int32, (tm, gdim), 0)
    for gi, win in enumerate(POOL_WINDOWS):
        cols = slice(gi * gdim, (gi + 1) * gdim)
        x = u_ref[:, cols]
        s = jnp.concatenate([hl[:, cols], x], axis=0)
        sh = 1
        while sh < win:
            s = s + pltpu.roll(s, sh, 0)
            sh *= 2
        cnt = jnp.minimum(pos + 1, win).astype(F32)
        pooled = s[halo:, :] / cnt - x
        y = jnp.dot(pooled.astype(BF16), w_ref[gi], preferred_element_type=F32)
        o_ref[:, cols] = (y * sc_ref[:, cols]).astype(o_ref.dtype)


def _pool(u, pool_w, pool_scale, *, tm, name):
    rows = u.shape[0]
    groups, gdim, _ = pool_w.shape
    cw = groups * gdim
    halo = 2 * HALO_F32
    hb = tm // halo
    return pl.pallas_call(
        functools.partial(_pool_body, tm=tm, gdim=gdim),
        grid=(rows // tm,),
        in_specs=[pl.BlockSpec((tm, cw), lambda i: (i, 0)),
                  pl.BlockSpec((halo, cw), lambda i: (jnp.maximum(i * hb - 1, 0), 0)),
                  pl.BlockSpec((groups, gdim, gdim), lambda i: (0, 0, 0)),
                  pl.BlockSpec((1, cw), lambda i: (0, 0))],
        out_specs=pl.BlockSpec((tm, cw), lambda i: (i, 0)),
        out_shape=jax.ShapeDtypeStruct((rows, cw), BF16),
        compiler_params=_cp(("parallel",)),
        name=name,
    )(u, u, pool_w.astype(BF16), pool_scale.reshape(1, cw))


def _split3_dot(x, sel):
    hi = x.astype(BF16)
    r1 = x - hi.astype(F32)
    mid = r1.astype(BF16)
    lo = (r1 - mid.astype(F32)).astype(BF16)
    return (jnp.dot(hi, sel, preferred_element_type=F32) + jnp.dot(mid, sel, preferred_element_type=F32)
            + jnp.dot(lo, sel, preferred_element_type=F32))


def _ssd_body(xs_ref, xh_ref, b_ref, bh_ref, c_ref, ch_ref, dt_ref, z_ref, cwx_ref, cwb_ref, cwc_ref,
              cbx_ref, cbb_ref, cbc_ref, dtb_ref, ah_ref, dsk_ref, o_ref, st_ref, *, nsub, hpg):
    g = pl.program_id(0)
    first = pl.program_id(1) == 0

    @pl.when(first)
    def _():
        st_ref[...] = jnp.zeros_like(st_ref)

    t = SCAN_T
    gw = hpg * D_HEAD_DIM

    def conv(x_ref, halo_ref, cw_ref, cb_ref):
        hl = halo_ref[...]
        x = jnp.concatenate([jnp.where(first, jnp.zeros_like(hl), hl), x_ref[...]], axis=0)
        cw = cw_ref[...]
        y = cw[D_CONV - 1:D_CONV, :] * x + cb_ref[...]
        for kk in range(1, D_CONV):
            y = y + cw[D_CONV - 1 - kk:D_CONV - kk, :] * pltpu.roll(x, kk, 0)
        return jax.nn.silu(y[HALO_F32:, :])

    xs = conv(xs_ref, xh_ref, cwx_ref, cbx_ref)
    bb = conv(b_ref, bh_ref, cwb_ref, cbb_ref).astype(BF16)
    cc = conv(c_ref, ch_ref, cwc_ref, cbc_ref).astype(BF16)
    dt = jax.nn.softplus(dt_ref[...] + dtb_ref[...])
    acs_h = _cumsum_chunks(dt * ah_ref[...])
    hrow = lax.broadcasted_iota(jnp.int32, (LANES, gw), 0)
    hcol = lax.broadcasted_iota(jnp.int32, (LANES, gw), 1) // D_HEAD_DIM + g * hpg
    sel = (hrow == hcol).astype(BF16)
    dtx = _split3_dot(dt, sel)
    acs = _split3_dot(acs_h, sel)
    xd = xs * dtx
    decay_in = jnp.exp(acs)
    last = [acs[c * t + t - 1:c * t + t, :] for c in range(nsub)]
    xdd = (xd * jnp.exp(_chunk_rows(last, t) - acs)).astype(BF16)

    pr = lax.broadcasted_iota(jnp.int32, (t, 2 * t), 0)
    pc = lax.broadcasted_iota(jnp.int32, (t, 2 * t), 1)
    pj = pc % t
    tril2 = pj <= pr
    eye2f = (pr == pj).astype(F32)
    left = pc < t

    diag, upd = [], []
    for c in range(nsub):
        rs = slice(c * t, (c + 1) * t)
        bc = bb[rs, :]
        cb2 = lax.dot_general(cc[rs, :], jnp.concatenate([bc, bc], axis=0), NT_DIMS, preferred_element_type=F32)
        parts = []
        for p in range(hpg // 2):
            cols = slice(p * 2 * D_HEAD_DIM, (p + 1) * 2 * D_HEAD_DIM)
            ac = acs[rs, cols]
            row_cs = jnp.sum(ac * eye2f, axis=0, keepdims=True)
            lmat = jnp.where(tril2, jnp.exp(ac - row_cs), 0.0)
            xdp = xd[rs, cols]
            rhs = jnp.concatenate([jnp.where(left, xdp, 0.0), jnp.where(left, 0.0, xdp)], axis=0)
            parts.append(jnp.dot((cb2 * lmat).astype(BF16), rhs.astype(BF16), preferred_element_type=F32))
        diag.append(jnp.concatenate(parts, axis=1))
        upd.append(lax.dot_general(bc, xdd[rs, :], TN_DIMS, preferred_element_type=F32))

    st = st_ref[...]
    ys = []
    for c in range(nsub):
        rs = slice(c * t, (c + 1) * t)
        ys.append(diag[c] + decay_in[rs, :] * jnp.dot(cc[rs, :], st.astype(BF16), preferred_element_type=F32))
        st = st * jnp.exp(last[c]) + upd[c]
    st_ref[...] = st
    y = jnp.concatenate(ys, axis=0) + xs * dsk_ref[...]
    o_ref[...] = y * jax.nn.silu(z_ref[...])


def _ssd(xbc, dt, z, conv_w, conv_b, dt_bias, a_h, d_ch, *, heads, tb, name):
    rows, ch = xbc.shape
    inner = heads * D_HEAD_DIM
    hpg = heads // D_GROUPS
    gw = hpg * D_HEAD_DIM
    assert hpg % 2 == 0 and 2 * D_HEAD_DIM == LANES and SCAN_T == D_HEAD_DIM
    nxb = inner // LANES
    hb = tb // HALO_F32
    prev = lambda c: jnp.maximum(c * hb - 1, 0)
    col_x = lambda g: g
    col_b = lambda g: nxb + g
    col_c = lambda g: nxb + D_GROUPS + g

    def data(width, col):
        return [pl.BlockSpec((tb, width), lambda g, c: (c, col(g))),
                pl.BlockSpec((HALO_F32, width), lambda g, c: (prev(c), col(g)))]

    def per_col(nrows, width, col):
        return pl.BlockSpec((nrows, width), lambda g, c: (0, col(g)))

    cb = conv_b.reshape(1, ch)
    return pl.pallas_call(
        functools.partial(_ssd_body, nsub=tb // SCAN_T, hpg=hpg),
        grid=(D_GROUPS, rows // tb),
        in_specs=(data(gw, col_x) + data(D_STATE, col_b) + data(D_STATE, col_c)
                  + [pl.BlockSpec((tb, LANES), lambda g, c: (c, 0)),
                     pl.BlockSpec((tb, gw), lambda g, c: (c, g)),
                     per_col(D_CONV, gw, col_x), per_col(D_CONV, D_STATE, col_b), per_col(D_CONV, D_STATE, col_c),
                     per_col(1, gw, col_x), per_col(1, D_STATE, col_b), per_col(1, D_STATE, col_c),
                     pl.BlockSpec((1, LANES), lambda g, c: (0, 0)),
                     pl.BlockSpec((1, LANES), lambda g, c: (0, 0)),
                     per_col(1, gw, col_x)]),
        out_specs=pl.BlockSpec((tb, gw), lambda g, c: (c, g)),
        out_shape=jax.ShapeDtypeStruct((rows, inner), F32),
        scratch_shapes=[pltpu.VMEM((D_STATE, gw), F32)],
        compiler_params=_cp(("parallel", "arbitrary")),
        name=name,
    )(xbc, xbc, xbc, xbc, xbc, xbc, dt, z, conv_w, conv_w, conv_w, cb, cb, cb, dt_bias, a_h, d_ch)


def _with_rot_half(w):
    half = B_ROPE // 2
    return jnp.concatenate([w, -w[..., half:], w[..., :half]], axis=-1)


def _layout_w_uq(w, heads):
    k = w.shape[0]
    w3 = w.astype(BF16).reshape(k, heads, B_NOPE + B_ROPE)
    nope = w3[:, :, :B_NOPE].reshape(k, heads * B_NOPE)
    pe = _with_rot_half(w3[:, :, B_NOPE:]).reshape(k, heads * LANES)
    return jnp.concatenate([nope, pe], axis=1)[None]


def kernel(x, meta_tokens, lb_logits, norm_g, ab_w_in, hgrn_norm_g, mla_q_norm_g, mla_kv_norm_g,
           mla_w_uq, mla_w_ukv, ab_w_out, cd_w_in, pool_w, pool_scale, ssm_conv_w, ssm_conv_b,
           ssm_dt_bias, ssm_a_log, ssm_d, ssm_norm_g, cd_w_out, ffn_w_up, ffn_conv_w, ffn_conv_b,
           ffn_w_down):
    bsz, seq, d = x.shape
    assert bsz == 1
    depth = norm_g.shape[0]
    total = N_META + seq
    lp = -(-total // Q_BLOCK) * Q_BLOCK
    a_heads = hgrn_norm_g.shape[1]
    a_kw = a_heads * A_DK
    a_w = a_heads * A_DV
    b_heads = mla_w_uq.shape[2] // (B_NOPE + B_ROPE)
    c_width = pool_scale.shape[1]
    d_inner = ssm_norm_g.shape[1]
    d_heads = ssm_dt_bias.shape[1]
    d_xbc = ssm_conv_w.shape[2]
    d_ff = ffn_w_down.shape[1]

    tm_mm = _pick(lp, (1040, 640, 512, 256, 128))
    tm_mid = _pick(lp, (640, 512, 256, 128))
    tm_row = _pick(lp, (320, 256, 128))
    tb = _pick(lp, (320, 256, 128, 64))
    tq = _pick(lp, (640, 512, 256, 128))
    tn_ff = _pick(d_ff, (256, 128))

    h = jnp.concatenate([meta_tokens.astype(x.dtype), x[0],
                         jnp.zeros((lp - total, d), x.dtype)], axis=0)

    inv = ROPE_THETA ** (-jnp.arange(0, B_ROPE, 2, dtype=F32) / B_ROPE)
    ang = jnp.arange(lp, dtype=F32)[:, None] * inv[None, :]
    cos, sin = jnp.cos(ang), jnp.sin(ang)
    cs = jnp.concatenate([cos, cos, sin, sin], axis=-1)
    lb_all = jnp.cumsum(jax.nn.softmax(lb_logits.astype(F32), axis=0), axis=0)
    w_up = ffn_w_up.astype(BF16)
    w_down = ffn_w_down.astype(BF16)

    xn = _rownorm(h, norm_g[0, 0], width=d, tm=tm_row, name="norm_in")

    for layer in range(depth):
        g = norm_g[layer]
        j = layer // 2
        xp = [(xn, d, 0, 0)]
        if layer % 2 == 0:
            n_ab = 2 * a_kw + 2 * a_w
            n_lora = B_Q_LORA + B_KV_LORA
            proj_a = _matmul(xp, ab_w_in, layer=j, col0=0, ncols=n_ab, tm=tm_mm, tn=512, out_dtype=F32,
                             name="ab_in_a")
            proj_c = _matmul(xp, ab_w_in, layer=j, col0=n_ab, ncols=n_lora, tm=tm_mm, tn=512, out_dtype=F32,
                             name="ab_in_c")
            w_kr = _take_cols(ab_w_in, layer=j, c0=n_ab + n_lora, width=B_ROPE, tr=256, name="w_kr")
            w_kr = _with_rot_half(w_kr.astype(BF16))[None]
            krp = _matmul(xp, w_kr, ncols=LANES, tm=tm_mm, tn=LANES, out_dtype=F32, name="ab_in_kr")
            o_a = _hgrn2(proj_a, lb_all[layer], hgrn_norm_g[j], heads=a_heads, tb=tb, name="hgrn2")
            cq = _rownorm(proj_c, mla_q_norm_g[j], width=B_Q_LORA, xblk=0, tm=tm_row, name="q_norm")
            ckv = _rownorm(proj_c, mla_kv_norm_g[j], width=B_KV_LORA, xblk=B_Q_LORA // B_KV_LORA, tm=tm_row,
                           name="kv_norm")
            w_uq = _layout_w_uq(mla_w_uq[j], b_heads)
            qf = _matmul([(cq, B_Q_LORA, 0, 0)], w_uq, ncols=w_uq.shape[2], tm=tm_mm, tn=512, out_dtype=F32,
                         name="uq")
            w_kv = mla_w_ukv[j].astype(BF16).reshape(B_KV_LORA, b_heads, 2, B_NOPE)
            w_k = w_kv[:, :, 0, :].reshape(1, B_KV_LORA, b_heads * B_NOPE)
            w_vt = w_kv[:, :, 1, :].reshape(B_KV_LORA, b_heads * B_DV).T
            kn = _matmul([(ckv, B_KV_LORA, 0, 0)], w_k, ncols=b_heads * B_NOPE, tm=tm_mm, tn=512,
                         out_dtype=BF16, name="uk")
            vt = _matmul_nt(w_vt, ckv, tm=tm_mid, tn=512, name="uv_t")
            kp = _kprep(krp, cs, tm=tm_row, name="k_rope")
            o_b = _attention(qf, cs, kn, kp, vt, heads=b_heads, tq=tq, tk=tq, name="attention")
            half = a_w
            mix = [(o_a, half, 0, 0), (o_b, half, 0, 1)]
            w_out, w_out_layer = ab_w_out, j
        else:
            n_zu = d_inner + c_width
            u_c = _matmul(xp, cd_w_in, layer=j, col0=0, ncols=c_width, tm=tm_mm, tn=512, out_dtype=F32,
                          name="cd_in_u")
            z = _matmul(xp, cd_w_in, layer=j, col0=c_width, ncols=d_inner, tm=tm_mm, tn=512, out_dtype=F32,
                        name="cd_in_z")
            xbc = _matmul(xp, cd_w_in, layer=j, col0=n_zu, ncols=d_xbc, tm=tm_mm, tn=512, out_dtype=F32,
                          name="cd_in_xbc")
            dt_pad = LANES - d_heads
            w_dt = _take_cols(cd_w_in, layer=j, c0=n_zu + d_xbc, width=d_heads, tr=256, name="w_dt")
            w_dt = jnp.pad(w_dt.astype(BF16), ((0, 0), (0, dt_pad)))[None]
            dt = _matmul(xp, w_dt, ncols=LANES, tm=tm_mm, tn=LANES, out_dtype=F32, name="cd_in_dt")
            o_c = _pool(u_c, pool_w[j], pool_scale[j], tm=tm_row, name="pool")
            a_h = jnp.pad(-jnp.exp(ssm_a_log[j].astype(F32)), (0, dt_pad)).reshape(1, LANES)
            d_ch = jnp.repeat(ssm_d[j].astype(F32), D_HEAD_DIM).reshape(1, d_inner)
            dtb = jnp.pad(ssm_dt_bias[j].astype(F32), (0, dt_pad)).reshape(1, LANES)
            y = _ssd(xbc, dt, z, ssm_conv_w[j], ssm_conv_b[j], dtb, a_h, d_ch, heads=d_heads, tb=tb, name="ssd")
            yn = _rownorm(y, ssm_norm_g[j], width=d_inner, tm=tm_row, name="ssd_norm")
            mix = [(o_c, c_width, 0, 0)] + [(yn, c_width, b, b + 1) for b in range(d_inner // c_width)]
            w_out, w_out_layer = cd_w_out, j

        mixed = _matmul(mix, w_out, layer=w_out_layer, ncols=d, tm=tm_mm, tn=512, out_dtype=F32, name="mix_out")
        h, xn = _addnorm(h, mixed, g[1], g[2], tm=tm_row, name="add_norm_mix")
        act = _ffn_up(xn, ffn_w_up, ffn_conv_w, ffn_conv_b, layer=layer, tm=tm_mm, tn=tn_ff, name="ffn_up")
        ff = _matmul([(act, d_ff, 0, 0)], w_down, layer=layer, ncols=d, tm=tm_mid, tn=256, out_dtype=F32,
                     name="ffn_down")
        if layer + 1 < depth:
            h, xn = _addnorm(h, ff, g[3], norm_g[layer + 1, 0], tm=tm_row, name="add_norm_ffn")

    out = _add_rows(h, ff, norm_g[depth - 1, 3], row0=N_META, nrows=seq, tm=_pick(seq, (256, 240, 128, 16, 8)),
                    name="add_out")
    return out[None]
```

```python
import functools
import math

import jax
import jax.numpy as jnp
from jax import lax
from jax.experimental import pallas as pl
from jax.experimental.pallas import tpu as pltpu

F32 = jnp.float32
BF16 = jnp.bfloat16
HIGHEST = lax.Precision.HIGHEST

CHUNK = 64
N_META = 16
Q_BLOCK = 128
NORM_EPS = 1e-6
A_DK = 128
A_DV = 128
B_NOPE = 128
B_ROPE = 64
B_DV = 128
B_Q_LORA = 1024
B_KV_LORA = 512
ROPE_THETA = 10000.0
POOL_WINDOWS = (2, 4, 8, 16)
D_HEAD_DIM = 64
D_GROUPS = 8
D_STATE = 128
D_CONV = 4
FFN_CONV = 3

SCAN_T = 64
SUB_T = 16
LANES = 128
HALO_BF16 = 16
HALO_F32 = 8
MIB = 1024 * 1024

NT_DIMS = (((1,), (1,)), ((), ()))
TN_DIMS = (((0,), (0,)), ((), ()))


def _cp(sem, vmem_mib=48):
    return pltpu.CompilerParams(dimension_semantics=sem, vmem_limit_bytes=vmem_mib * MIB)


def _pick(n, cands):
    for c in cands:
        if n % c == 0:
            return c
    raise ValueError(f"no tile for {n} in {cands}")


def _nchunks(rows, most=6):
    tiles = rows // HALO_BF16
    return max(c for c in range(1, most + 1) if tiles % c == 0)


def _rms(x, g):
    return x * lax.rsqrt(jnp.mean(x * x, axis=-1, keepdims=True) + NORM_EPS) * g


def _mm_body(*refs, n_lhs, wt):
    o_ref = refs[-1]
    acc = None
    for x_ref, w_ref in zip(refs[:n_lhs], refs[n_lhs:2 * n_lhs]):
        w = w_ref[...].astype(BF16)
        if wt:
            part = lax.dot_general(x_ref[...], w, NT_DIMS, preferred_element_type=F32)
        else:
            part = jnp.dot(x_ref[...], w, preferred_element_type=F32)
        acc = part if acc is None else acc + part
    o_ref[...] = acc.astype(o_ref.dtype)


def _matmul(pieces, w, *, layer=0, col0=0, ncols, tm, tn, out_dtype, name, wt=False, vmem_mib=56):
    rows = pieces[0][0].shape[0]
    assert col0 % tn == 0 and ncols % tn == 0 and rows % tm == 0
    cb = col0 // tn
    x_specs = [pl.BlockSpec((tm, k), lambda i, j, xb=xb: (i, xb)) for _, k, xb, _ in pieces]
    if wt:
        w_specs = [pl.BlockSpec((pl.Squeezed(), tn, k), lambda i, j, wb=wb: (layer, cb + j, wb))
                   for _, k, _, wb in pieces]
    else:
        w_specs = [pl.BlockSpec((pl.Squeezed(), k, tn), lambda i, j, wb=wb: (layer, wb, cb + j))
                   for _, k, _, wb in pieces]
    return pl.pallas_call(
        functools.partial(_mm_body, n_lhs=len(pieces), wt=wt),
        grid=(rows // tm, ncols // tn),
        in_specs=x_specs + w_specs,
        out_specs=pl.BlockSpec((tm, tn), lambda i, j: (i, j)),
        out_shape=jax.ShapeDtypeStruct((rows, ncols), out_dtype),
        compiler_params=_cp(("parallel", "parallel"), vmem_mib),
        name=name,
    )(*[p[0] for p in pieces], *([w] * len(pieces)))


def _mm_nt_body(w_ref, x_ref, o_ref):
    o_ref[...] = lax.dot_general(w_ref[...], x_ref[...], NT_DIMS,
                                 preferred_element_type=F32).astype(o_ref.dtype)


def _matmul_nt(wt, x, *, tm, tn, name):
    n, k = wt.shape
    rows = x.shape[0]
    return pl.pallas_call(
        _mm_nt_body,
        grid=(n // tn, rows // tm),
        in_specs=[pl.BlockSpec((tn, k), lambda a, b: (a, 0)),
                  pl.BlockSpec((tm, k), lambda a, b: (b, 0))],
        out_specs=pl.BlockSpec((tn, tm), lambda a, b: (a, b)),
        out_shape=jax.ShapeDtypeStruct((n, rows), BF16),
        compiler_params=_cp(("parallel", "parallel")),
        name=name,
    )(wt, x)


def _rownorm_body(x_ref, g_ref, o_ref):
    o_ref[...] = _rms(x_ref[...].astype(F32), g_ref[...]).astype(o_ref.dtype)


def _rownorm(x, g, *, width, xblk=0, tm, name):
    rows = x.shape[0]
    return pl.pallas_call(
        _rownorm_body,
        grid=(rows // tm,),
        in_specs=[pl.BlockSpec((tm, width), lambda i: (i, xblk)),
                  pl.BlockSpec((1, width), lambda i: (0, 0))],
        out_specs=pl.BlockSpec((tm, width), lambda i: (i, 0)),
        out_shape=jax.ShapeDtypeStruct((rows, width), BF16),
        compiler_params=_cp(("parallel",)),
        name=name,
    )(x, g.reshape(1, width).astype(F32))


def _addnorm_body(h_ref, m_ref, g1_ref, g2_ref, ho_ref, xo_ref):
    hn = h_ref[...] + _rms(m_ref[...], g1_ref[...])
    ho_ref[...] = hn
    xo_ref[...] = _rms(hn, g2_ref[...]).astype(xo_ref.dtype)


def _add_body(h_ref, m_ref, g1_ref, ho_ref):
    ho_ref[...] = h_ref[...] + _rms(m_ref[...], g1_ref[...])


def _add_rows(h, m, g_post, *, row0, nrows, tm, name):
    d = h.shape[1]
    assert row0 % HALO_F32 == 0 and tm % HALO_F32 == 0
    win = pl.BlockSpec((pl.Element(tm), pl.Element(d)),
                       lambda i: (pl.multiple_of(row0 + i * tm, HALO_F32), 0))
    return pl.pallas_call(
        _add_body, grid=(nrows // tm,),
        in_specs=[win, win, pl.BlockSpec((1, d), lambda i: (0, 0))],
        out_specs=pl.BlockSpec((tm, d), lambda i: (i, 0)),
        out_shape=jax.ShapeDtypeStruct((nrows, d), F32),
        compiler_params=_cp(("parallel",)), name=name,
    )(h, m, g_post.reshape(1, d))


def _addnorm(h, m, g_post, g_next, *, tm, name):
    rows, d = h.shape
    row_spec = pl.BlockSpec((tm, d), lambda i: (i, 0))
    g_spec = pl.BlockSpec((1, d), lambda i: (0, 0))
    return pl.pallas_call(
        _addnorm_body, grid=(rows // tm,),
        in_specs=[row_spec, row_spec, g_spec, g_spec], out_specs=[row_spec, row_spec],
        out_shape=[jax.ShapeDtypeStruct((rows, d), F32), jax.ShapeDtypeStruct((rows, d), BF16)],
        compiler_params=_cp(("parallel",)), name=name,
    )(h, m, g_post.reshape(1, d), g_next.reshape(1, d))


def _ffn_up_body(x_ref, wg_ref, wv_ref, cwg_ref, cwv_ref, cbg_ref, cbv_ref, o_ref, tail_ref, *, n_mm):
    j = pl.program_id(1)

    @pl.when(pl.program_id(0) == 0)
    def _():
        tail_ref[j] = jnp.zeros(tail_ref.shape[1:], F32)

    mm_rows = x_ref.shape[0] // n_mm
    wg = wg_ref[...].astype(BF16)
    wv = wv_ref[...].astype(BF16)
    cwg, cwv = cwg_ref[...], cwv_ref[...]
    cbg, cbv = cbg_ref[...], cbv_ref[...]

    def matmuls(c):
        xc = x_ref[c * mm_rows:(c + 1) * mm_rows, :]
        return (jnp.dot(xc, wg, preferred_element_type=F32), jnp.dot(xc, wv, preferred_element_type=F32))

    def conv(u, above, cw, cb):
        u = jnp.concatenate([above, u], axis=0)
        y = cw[2:3, :] * u + cw[1:2, :] * pltpu.roll(u, 1, 0) + cw[0:1, :] * pltpu.roll(u, 2, 0) + cb
        return y[HALO_F32:, :]

    tail = lambda u: (u[0][mm_rows - HALO_F32:, :], u[1][mm_rows - HALO_F32:, :])
    carried = tail_ref[j]
    above = (carried[0], carried[1])
    u_cur = matmuls(0)
    for c in range(n_mm):
        u_next = matmuls(c + 1) if c + 1 < n_mm else None
        gate = conv(u_cur[0], above[0], cwg, cbg)
        val = conv(u_cur[1], above[1], cwv, cbv)
        o_ref[c * mm_rows:(c + 1) * mm_rows, :] = (jax.nn.silu(gate) * val).astype(o_ref.dtype)
        above = tail(u_cur)
        u_cur = u_next
    tail_ref[j] = jnp.stack(above)


def _ffn_up(xn, w_up, conv_w, conv_b, *, layer, tm, tn, name):
    rows, d = xn.shape
    d_ff = w_up.shape[2] // 2
    nj = d_ff // tn
    cb = conv_b.reshape(conv_b.shape[0], 1, 2 * d_ff)
    sq = pl.Squeezed()
    return pl.pallas_call(
        functools.partial(_ffn_up_body, n_mm=_nchunks(tm, most=8)),
        grid=(rows // tm, nj),
        in_specs=[pl.BlockSpec((tm, d), lambda i, j: (i, 0)),
                  pl.BlockSpec((sq, d, tn), lambda i, j: (layer, 0, j)),
                  pl.BlockSpec((sq, d, tn), lambda i, j: (layer, 0, nj + j)),
                  pl.BlockSpec((sq, FFN_CONV, tn), lambda i, j: (layer, 0, j)),
                  pl.BlockSpec((sq, FFN_CONV, tn), lambda i, j: (layer, 0, nj + j)),
                  pl.BlockSpec((sq, 1, tn), lambda i, j: (layer, 0, j)),
                  pl.BlockSpec((sq, 1, tn), lambda i, j: (layer, 0, nj + j))],
        out_specs=pl.BlockSpec((tm, tn), lambda i, j: (i, j)),
        out_shape=jax.ShapeDtypeStruct((rows, d_ff), BF16),
        scratch_shapes=[pltpu.VMEM((nj, 2, HALO_F32, tn), F32)],
        compiler_params=_cp(("arbitrary", "arbitrary"), 56),
        name=name,
    )(xn, w_up, w_up, conv_w, conv_w, cb, cb)


def _cumsum_chunks(x):
    row = lax.broadcasted_iota(jnp.int32, x.shape, 0) % SCAN_T
    sh = 1
    while sh < SCAN_T:
        x = x + jnp.where(row >= sh, pltpu.roll(x, sh, 0), 0.0)
        sh *= 2
    return x


def _chunk_rows(rows, reps):
    return jnp.concatenate([jnp.broadcast_to(r, (reps, r.shape[1])) for r in rows], axis=0)


def _hgrn2_body(q_ref, f_ref, v_ref, g_ref, lb_ref, hg_ref, o_ref, st_ref, *, nsub):
    @pl.when(pl.program_id(1) == 0)
    def _():
        st_ref[...] = jnp.zeros_like(st_ref)

    t = SCAN_T
    nblk = t // SUB_T
    lb = lb_ref[...]
    ri = lax.broadcasted_iota(jnp.int32, (t, t), 0)
    ci = lax.broadcasted_iota(jnp.int32, (t, t), 1)
    tril = ci <= ri

    q = q_ref[...]
    fp = f_ref[...]
    vb = v_ref[...].astype(BF16)
    row = lax.broadcasted_iota(jnp.int32, q.shape, 0) % t
    lf = jnp.log(lb + (1.0 - lb) * jax.nn.sigmoid(fp))
    k = (1.0 - lb) * jax.nn.sigmoid(-fp)
    b = _cumsum_chunks(lf)
    ref = [[b[c * t + u * SUB_T + SUB_T // 2 - 1:c * t + u * SUB_T + SUB_T // 2, :] for u in range(nblk)]
           for c in range(nsub)]
    last = [b[c * t + t - 1:c * t + t, :] for c in range(nsub)]
    r_own = _chunk_rows([ref[c][u] for c in range(nsub) for u in range(nblk)], SUB_T)
    qt = (q * jnp.exp(b - r_own)).astype(BF16)
    kh = [jnp.where(row < (u + 1) * SUB_T,
                    k * jnp.exp(_chunk_rows([ref[c][u] for c in range(nsub)], t) - b), 0.0).astype(BF16)
          for u in range(nblk)]
    kd = (k * jnp.exp(_chunk_rows(last, t) - b)).astype(BF16)
    qe = (q * jnp.exp(b)).astype(BF16)

    intra, upd = [], []
    for c in range(nsub):
        lo = c * t
        rows = [lax.dot_general(qt[lo + u * SUB_T:lo + (u + 1) * SUB_T, :], kh[u][lo:lo + t, :], NT_DIMS,
                                preferred_element_type=F32) for u in range(nblk)]
        a = jnp.where(tril, jnp.concatenate(rows, axis=0), 0.0).astype(BF16)
        intra.append(jnp.dot(a, vb[lo:lo + t, :], preferred_element_type=F32))
        upd.append(lax.dot_general(vb[lo:lo + t, :], kd[lo:lo + t, :], TN_DIMS, preferred_element_type=F32))

    st = st_ref[...]
    outs = []
    for c in range(nsub):
        outs.append(intra[c] + lax.dot_general(qe[c * t:(c + 1) * t, :], st.astype(BF16), NT_DIMS,
                                               preferred_element_type=F32))
        st = st * jnp.exp(last[c]) + upd[c]
    st_ref[...] = st
    o = _rms(jnp.concatenate(outs, axis=0), hg_ref[...])
    o_ref[...] = (o * jax.nn.silu(g_ref[...])).astype(o_ref.dtype)


def _hgrn2(proj, lb, hg, *, heads, tb, name):
    rows = proj.shape[0]
    blk = lambda off: pl.BlockSpec((tb, A_DK), lambda h, c, off=off: (c, off + h))
    vec = pl.BlockSpec((pl.Squeezed(), 1, A_DK), lambda h, c: (h, 0, 0))
    return pl.pallas_call(
        functools.partial(_hgrn2_body, nsub=tb // SCAN_T),
        grid=(heads, rows // tb),
        in_specs=[blk(0), blk(heads), blk(2 * heads), blk(3 * heads), vec, vec],
        out_specs=pl.BlockSpec((tb, A_DV), lambda h, c: (c, h)),
        out_shape=jax.ShapeDtypeStruct((rows, heads * A_DV), BF16),
        scratch_shapes=[pltpu.VMEM((A_DV, A_DK), F32)],
        compiler_params=_cp(("parallel", "arbitrary")),
        name=name,
    )(proj, proj, proj, proj, lb.reshape(heads, 1, A_DK), hg.reshape(heads, 1, A_DV))


def _kprep_body(kr_ref, cs_ref, o_ref):
    t = kr_ref[...] * cs_ref[...]
    o_ref[...] = (t + pltpu.roll(t, B_ROPE, 1)).astype(o_ref.dtype)


def _kprep(krp, cs, *, tm, name):
    rows = krp.shape[0]
    spec = pl.BlockSpec((tm, LANES), lambda i: (i, 0))
    return pl.pallas_call(
        _kprep_body, grid=(rows // tm,), in_specs=[spec, spec], out_specs=spec,
        out_shape=jax.ShapeDtypeStruct((rows, LANES), BF16),
        compiler_params=_cp(("parallel",)), name=name,
    )(krp, cs)


def _attn_body(qn_ref, qp_ref, cs_ref, kn_ref, kp_ref, vt_ref, o_ref, kcat_ref, *, tq, tk, n_kv, scale, hpb):
    i = pl.program_id(1)

    hs = [slice(h * LANES, (h + 1) * LANES) for h in range(hpb)]

    @pl.when(i == 0)
    def _():
        for h in range(hpb):
            kcat_ref[h, :, 0:B_NOPE] = kn_ref[:, hs[h]]
            kcat_ref[h, :, B_NOPE:] = kp_ref[...]

    qs = i * tq
    cs = cs_ref[...]
    qcat = [jnp.concatenate([(qn_ref[:, hs[h]] * scale).astype(BF16),
                             (qp_ref[:, hs[h]] * cs * scale).astype(BF16)], axis=1) for h in range(hpb)]
    off = CHUNK - N_META
    first_cid = (qs + off) // CHUNK
    last_cid = (qs + tq - 1 + off) // CHUNK
    n_full = jnp.minimum((CHUNK * first_cid + N_META) // tk, n_kv)
    n_vis = jnp.minimum((CHUNK * last_cid + N_META - 1) // tk + 1, n_kv)
    q_cid = (qs + lax.broadcasted_iota(jnp.int32, (tk, tq), 1) + off) // CHUNK
    k_iota = lax.broadcasted_iota(jnp.int32, (tk, tq), 0) + off

    def make_step(masked):
        def step(j, carry):
            ks = pl.multiple_of(j * tk, tk)
            s = [lax.dot_general(kcat_ref[h, pl.ds(ks, tk), :], qcat[h], NT_DIMS, preferred_element_type=F32)
                 for h in range(hpb)]
            out = []
            for h in range(hpb):
                m, l, acc = carry[h]
                sh = jnp.where((k_iota + ks) // CHUNK <= q_cid, s[h], -jnp.inf) if masked else s[h]
                m_new = jnp.maximum(m, jnp.max(sh, axis=0, keepdims=True))
                p = jnp.exp2(sh - m_new)
                alpha = jnp.exp2(m - m_new)
                l = alpha * l + jnp.sum(p, axis=0, keepdims=True)
                acc = alpha * acc + jnp.dot(vt_ref[hs[h], pl.ds(ks, tk)], p.astype(BF16),
                                            preferred_element_type=F32)
                out.append((m_new, l, acc))
            return tuple(out)
        return step

    init = tuple((jnp.full((1, tq), -jnp.inf, F32), jnp.zeros((1, tq), F32), jnp.zeros((B_DV, tq), F32))
                 for _ in range(hpb))
    carry = lax.fori_loop(0, n_full, make_step(False), init)
    carry = lax.fori_loop(n_full, n_vis, make_step(True), carry)
    for h in range(hpb):
        _, l, acc = carry[h]
        o_ref[:, hs[h]] = (acc / l).T.astype(o_ref.dtype)


def _attention(qf, cs, kn, kp, vt, *, heads, tq, tk, name, hpb=2):
    rows = qf.shape[0]
    scale = float(B_NOPE + B_ROPE) ** -0.5 * math.log2(math.e)
    wide = hpb * LANES
    groups = heads // hpb
    return pl.pallas_call(
        functools.partial(_attn_body, tq=tq, tk=tk, n_kv=rows // tk, scale=scale, hpb=hpb),
        grid=(groups, rows // tq),
        in_specs=[pl.BlockSpec((tq, wide), lambda h, i: (i, h)),
                  pl.BlockSpec((tq, wide), lambda h, i: (i, groups + h)),
                  pl.BlockSpec((tq, LANES), lambda h, i: (i, 0)),
                  pl.BlockSpec((rows, wide), lambda h, i: (0, h)),
                  pl.BlockSpec((rows, LANES), lambda h, i: (0, 0)),
                  pl.BlockSpec((wide, rows), lambda h, i: (h, 0))],
        out_specs=pl.BlockSpec((tq, wide), lambda h, i: (i, h)),
        out_shape=jax.ShapeDtypeStruct((rows, heads * B_DV), BF16),
        scratch_shapes=[pltpu.VMEM((hpb, rows, B_NOPE + LANES), BF16)],
        compiler_params=_cp(("parallel", "arbitrary")),
        name=name,
    )(qf, qf, cs, kn, kp, vt)


def _pool_body(u_ref, halo_ref, w_ref, sc_ref, o_ref, *, tm, gdim):
    i = pl.program_id(0)
    hl = halo_ref[...]
    hl = jnp.where(i == 0, jnp.zeros_like(hl), hl)
    halo = hl.shape[0]
    pos = i * tm + lax.broadcasted_iota(jnp.int32, (tm, gdim), 0)
    for gi, win in enumerate(POOL_WINDOWS):
        cols = slice(gi * gdim, (gi + 1) * gdim)
        x = u_ref[:, cols]
        s = jnp.concatenate([hl[:, cols], x], axis=0)
        sh = 1
        while sh < win:
            s = s + pltpu.roll(s, sh, 0)
            sh *= 2
        cnt = jnp.minimum(pos + 1, win).astype(F32)
        pooled = s[halo:, :] / cnt - x
        y = jnp.dot(pooled.astype(BF16), w_ref[gi], preferred_element_type=F32)
        o_ref[:, cols] = (y * sc_ref[:, cols]).astype(o_ref.dtype)


def _pool(u, pool_w, pool_scale, *, tm, name):
    rows = u.shape[0]
    groups, gdim, _ = pool_w.shape
    cw = groups * gdim
    halo = 2 * HALO_F32
    hb = tm // halo
    return pl.pallas_call(
        functools.partial(_pool_body, tm=tm, gdim=gdim),
        grid=(rows // tm,),
        in_specs=[pl.BlockSpec((tm, cw), lambda i: (i, 0)),
                  pl.BlockSpec((halo, cw), lambda i: (jnp.maximum(i * hb - 1, 0), 0)),
                  pl.BlockSpec((groups, gdim, gdim), lambda i: (0, 0, 0)),
                  pl.BlockSpec((1, cw), lambda i: (0, 0))],
        out_specs=pl.BlockSpec((tm, cw), lambda i: (i, 0)),
        out_shape=jax.ShapeDtypeStruct((rows, cw), BF16),
        compiler_params=_cp(("parallel",)),
        name=name,
    )(u, u, pool_w.astype(BF16), pool_scale.reshape(1, cw))


def _split3_dot(x, sel):
    hi = x.astype(BF16)
    r1 = x - hi.astype(F32)
    mid = r1.astype(BF16)
    lo = (r1 - mid.astype(F32)).astype(BF16)
    return (jnp.dot(hi, sel, preferred_element_type=F32) + jnp.dot(mid, sel, preferred_element_type=F32)
            + jnp.dot(lo, sel, preferred_element_type=F32))


def _ssd_body(xs_ref, xh_ref, b_ref, bh_ref, c_ref, ch_ref, dt_ref, z_ref, cwx_ref, cwb_ref, cwc_ref,
              cbx_ref, cbb_ref, cbc_ref, dtb_ref, ah_ref, dsk_ref, o_ref, st_ref, *, nsub, hpg):
    g = pl.program_id(0)
    first = pl.program_id(1) == 0

    @pl.when(first)
    def _():
        st_ref[...] = jnp.zeros_like(st_ref)

    t = SCAN_T
    gw = hpg * D_HEAD_DIM

    def conv(x_ref, halo_ref, cw_ref, cb_ref):
        hl = halo_ref[...]
        x = jnp.concatenate([jnp.where(first, jnp.zeros_like(hl), hl), x_ref[...]], axis=0)
        cw = cw_ref[...]
        y = cw[D_CONV - 1:D_CONV, :] * x + cb_ref[...]
        for kk in range(1, D_CONV):
            y = y + cw[D_CONV - 1 - kk:D_CONV - kk, :] * pltpu.roll(x, kk, 0)
        return jax.nn.silu(y[HALO_F32:, :])

    xs = conv(xs_ref, xh_ref, cwx_ref, cbx_ref)
    bb = conv(b_ref, bh_ref, cwb_ref, cbb_ref).astype(BF16)
    cc = conv(c_ref, ch_ref, cwc_ref, cbc_ref).astype(BF16)
    dt = jax.nn.softplus(dt_ref[...] + dtb_ref[...])
    acs_h = _cumsum_chunks(dt * ah_ref[...])
    hrow = lax.broadcasted_iota(jnp.int32, (LANES, gw), 0)
    hcol = lax.broadcasted_iota(jnp.int32, (LANES, gw), 1) // D_HEAD_DIM + g * hpg
    sel = (hrow == hcol).astype(BF16)
    dtx = _split3_dot(dt, sel)
    acs = _split3_dot(acs_h, sel)
    xd = xs * dtx
    decay_in = jnp.exp(acs)
    last = [acs[c * t + t - 1:c * t + t, :] for c in range(nsub)]
    xdd = (xd * jnp.exp(_chunk_rows(last, t) - acs)).astype(BF16)

    pr = lax.broadcasted_iota(jnp.int32, (t, 2 * t), 0)
    pc = lax.broadcasted_iota(jnp.int32, (t, 2 * t), 1)
    pj = pc % t
    tril2 = pj <= pr
    eye2f = (pr == pj).astype(F32)
    left = pc < t

    diag, upd = [], []
    for c in range(nsub):
        rs = slice(c * t, (c + 1) * t)
        bc = bb[rs, :]
        cb2 = lax.dot_general(cc[rs, :], jnp.concatenate([bc, bc], axis=0), NT_DIMS, preferred_element_type=F32)
        parts = []
        for p in range(hpg // 2):
            cols = slice(p * 2 * D_HEAD_DIM, (p + 1) * 2 * D_HEAD_DIM)
            ac = acs[rs, cols]
            row_cs = jnp.sum(ac * eye2f, axis=0, keepdims=True)
            lmat = jnp.where(tril2, jnp.exp(ac - row_cs), 0.0)
            xdp = xd[rs, cols]
            rhs = jnp.concatenate([jnp.where(left, xdp, 0.0), jnp.where(left, 0.0, xdp)], axis=0)
            parts.append(jnp.dot((cb2 * lmat).astype(BF16), rhs.astype(BF16), preferred_element_type=F32))
        diag.append(jnp.concatenate(parts, axis=1))
        upd.append(lax.dot_general(bc, xdd[rs, :], TN_DIMS, preferred_element_type=F32))

    st = st_ref[...]
    ys = []
    for c in range(nsub):
        rs = slice(c * t, (c + 1) * t)
        ys.append(diag[c] + decay_in[rs, :] * jnp.dot(cc[rs, :], st.astype(BF16), preferred_element_type=F32))
        st = st * jnp.exp(last[c]) + upd[c]
    st_ref[...] = st
    y = jnp.concatenate(ys, axis=0) + xs * dsk_ref[...]
    o_ref[...] = y * jax.nn.silu(z_ref[...])


def _ssd(xbc, dt, z, conv_w, conv_b, dt_bias, a_h, d_ch, *, heads, tb, name):
    rows, ch = xbc.shape
    inner = heads * D_HEAD_DIM
    hpg = heads // D_GROUPS
    gw = hpg * D_HEAD_DIM
    assert hpg % 2 == 0 and 2 * D_HEAD_DIM == LANES and SCAN_T == D_HEAD_DIM
    nxb = inner // LANES
    hb = tb // HALO_F32
    prev = lambda c: jnp.maximum(c * hb - 1, 0)
    col_x = lambda g: g
    col_b = lambda g: nxb + g
    col_c = lambda g: nxb + D_GROUPS + g

    def data(width, col):
        return [pl.BlockSpec((tb, width), lambda g, c: (c, col(g))),
                pl.BlockSpec((HALO_F32, width), lambda g, c: (prev(c), col(g)))]

    def per_col(nrows, width, col):
        return pl.BlockSpec((nrows, width), lambda g, c: (0, col(g)))

    cb = conv_b.reshape(1, ch)
    return pl.pallas_call(
        functools.partial(_ssd_body, nsub=tb // SCAN_T, hpg=hpg),
        grid=(D_GROUPS, rows // tb),
        in_specs=(data(gw, col_x) + data(D_STATE, col_b) + data(D_STATE, col_c)
                  + [pl.BlockSpec((tb, LANES), lambda g, c: (c, 0)),
                     pl.BlockSpec((tb, gw), lambda g, c: (c, g)),
                     per_col(D_CONV, gw, col_x), per_col(D_CONV, D_STATE, col_b), per_col(D_CONV, D_STATE, col_c),
                     per_col(1, gw, col_x), per_col(1, D_STATE, col_b), per_col(1, D_STATE, col_c),
                     pl.BlockSpec((1, LANES), lambda g, c: (0, 0)),
                     pl.BlockSpec((1, LANES), lambda g, c: (0, 0)),
                     per_col(1, gw, col_x)]),
        out_specs=pl.BlockSpec((tb, gw), lambda g, c: (c, g)),
        out_shape=jax.ShapeDtypeStruct((rows, inner), F32),
        scratch_shapes=[pltpu.VMEM((D_STATE, gw), F32)],
        compiler_params=_cp(("parallel", "arbitrary")),
        name=name,
    )(xbc, xbc, xbc, xbc, xbc, xbc, dt, z, conv_w, conv_w, conv_w, cb, cb, cb, dt_bias, a_h, d_ch)


def _with_rot_half(w):
    half = B_ROPE // 2
    return jnp.concatenate([w, -w[..., half:], w[..., :half]], axis=-1)


def _layout_w_uq(w, heads):
    k = w.shape[0]
    w3 = w.astype(BF16).reshape(k, heads, B_NOPE + B_ROPE)
    nope = w3[:, :, :B_NOPE].reshape(k, heads * B_NOPE)
    pe = _with_rot_half(w3[:, :, B_NOPE:]).reshape(k, heads * LANES)
    return jnp.concatenate([nope, pe], axis=1)[None]


def kernel(x, meta_tokens, lb_logits, norm_g, ab_w_in, hgrn_norm_g, mla_q_norm_g, mla_kv_norm_g,
           mla_w_uq, mla_w_ukv, ab_w_out, cd_w_in, pool_w, pool_scale, ssm_conv_w, ssm_conv_b,
           ssm_dt_bias, ssm_a_log, ssm_d, ssm_norm_g, cd_w_out, ffn_w_up, ffn_conv_w, ffn_conv_b,
           ffn_w_down):
    bsz, seq, d = x.shape
    assert bsz == 1
    depth = norm_g.shape[0]
    total = N_META + seq
    lp = -(-total // Q_BLOCK) * Q_BLOCK
    a_heads = hgrn_norm_g.shape[1]
    a_kw = a_heads * A_DK
    a_w = a_heads * A_DV
    b_heads = mla_w_uq.shape[2] // (B_NOPE + B_ROPE)
    c_width = pool_scale.shape[1]
    d_inner = ssm_norm_g.shape[1]
    d_heads = ssm_dt_bias.shape[1]
    d_xbc = ssm_conv_w.shape[2]
    d_ff = ffn_w_down.shape[1]

    tm_mm = _pick(lp, (1040, 640, 512, 256, 128))
    tm_mid = _pick(lp, (640, 512, 256, 128))
    tm_row = _pick(lp, (320, 256, 128))
    tb = _pick(lp, (320, 256, 128, 64))
    tq = _pick(lp, (640, 512, 256, 128))
    tn_ff = _pick(d_ff, (256, 128))
    tm_ff = _pick(lp, (1664, 1040, 640, 512, 256, 128))

    h = jnp.concatenate([meta_tokens.astype(x.dtype), x[0],
                         jnp.zeros((lp - total, d), x.dtype)], axis=0)

    inv = ROPE_THETA ** (-jnp.arange(0, B_ROPE, 2, dtype=F32) / B_ROPE)
    ang = jnp.arange(lp, dtype=F32)[:, None] * inv[None, :]
    cos, sin = jnp.cos(ang), jnp.sin(ang)
    cs = jnp.concatenate([cos, cos, sin, sin], axis=-1)
    lb_all = jnp.cumsum(jax.nn.softmax(lb_logits.astype(F32), axis=0), axis=0)
    w_up = ffn_w_up.astype(BF16)
    w_down = ffn_w_down.astype(BF16)

    xn = _rownorm(h, norm_g[0, 0], width=d, tm=tm_row, name="norm_in")

    for layer in range(depth):
        g = norm_g[layer]
        j = layer // 2
        xp = [(xn, d, 0, 0)]
        if layer % 2 == 0:
            n_ab = 2 * a_kw + 2 * a_w
            n_lora = B_Q_LORA + B_KV_LORA
            w_t = jnp.swapaxes(ab_w_in, 1, 2)
            proj_a = _matmul(xp, w_t, layer=j, col0=0, ncols=n_ab, tm=tm_mm, tn=512, out_dtype=F32, wt=True,
                             name="ab_in_a")
            proj_c = _matmul(xp, w_t, layer=j, col0=n_ab, ncols=n_lora, tm=tm_mm, tn=512, out_dtype=F32, wt=True,
                             name="ab_in_c")
            w_kr = _with_rot_half(w_t[j, n_ab + n_lora:, :].astype(BF16).T).T[None]
            krp = _matmul(xp, w_kr, ncols=LANES, tm=tm_mm, tn=LANES, out_dtype=F32, wt=True, name="ab_in_kr")
            o_a = _hgrn2(proj_a, lb_all[layer], hgrn_norm_g[j], heads=a_heads, tb=tb, name="hgrn2")
            cq = _rownorm(proj_c, mla_q_norm_g[j], width=B_Q_LORA, xblk=0, tm=tm_row, name="q_norm")
            ckv = _rownorm(proj_c, mla_kv_norm_g[j], width=B_KV_LORA, xblk=B_Q_LORA // B_KV_LORA, tm=tm_row,
                           name="kv_norm")
            w_uq = _layout_w_uq(mla_w_uq[j], b_heads)
            qf = _matmul([(cq, B_Q_LORA, 0, 0)], w_uq, ncols=w_uq.shape[2], tm=tm_mm, tn=512, out_dtype=F32,
                         name="uq")
            w_kv = mla_w_ukv[j].astype(BF16).reshape(B_KV_LORA, b_heads, 2, B_NOPE)
            w_k = w_kv[:, :, 0, :].reshape(1, B_KV_LORA, b_heads * B_NOPE)
            w_vt = w_kv[:, :, 1, :].reshape(B_KV_LORA, b_heads * B_DV).T
            kn = _matmul([(ckv, B_KV_LORA, 0, 0)], w_k, ncols=b_heads * B_NOPE, tm=tm_mm, tn=512,
                         out_dtype=BF16, name="uk")
            vt = _matmul_nt(w_vt, ckv, tm=tm_mid, tn=512, name="uv_t")
            kp = _kprep(krp, cs, tm=tm_row, name="k_rope")
            o_b = _attention(qf, cs, kn, kp, vt, heads=b_heads, tq=tq, tk=tq, name="attention")
            half = a_w
            mix = [(o_a, half, 0, 0), (o_b, half, 0, 1)]
            w_out, w_out_layer = ab_w_out, j
        else:
            n_zu = d_inner + c_width
            w_t = jnp.swapaxes(cd_w_in, 1, 2)
            u_c = _matmul(xp, w_t, layer=j, col0=0, ncols=c_width, tm=tm_mm, tn=512, out_dtype=F32, wt=True,
                          name="cd_in_u")
            z = _matmul(xp, w_t, layer=j, col0=c_width, ncols=d_inner, tm=tm_mm, tn=512, out_dtype=F32, wt=True,
                        name="cd_in_z")
            xbc = _matmul(xp, w_t, layer=j, col0=n_zu, ncols=d_xbc, tm=tm_mm, tn=512, out_dtype=F32, wt=True,
                          name="cd_in_xbc")
            dt_pad = LANES - d_heads
            w_dt = jnp.pad(w_t[j, n_zu + d_xbc:, :].astype(BF16), ((0, dt_pad), (0, 0)))[None]
            dt = _matmul(xp, w_dt, ncols=LANES, tm=tm_mm, tn=LANES, out_dtype=F32, wt=True, name="cd_in_dt")
            o_c = _pool(u_c, pool_w[j], pool_scale[j], tm=tm_row, name="pool")
            a_h = jnp.pad(-jnp.exp(ssm_a_log[j].astype(F32)), (0, dt_pad)).reshape(1, LANES)
            d_ch = jnp.repeat(ssm_d[j].astype(F32), D_HEAD_DIM).reshape(1, d_inner)
            dtb = jnp.pad(ssm_dt_bias[j].astype(F32), (0, dt_pad)).reshape(1, LANES)
            y = _ssd(xbc, dt, z, ssm_conv_w[j], ssm_conv_b[j], dtb, a_h, d_ch, heads=d_heads, tb=tb, name="ssd")
            yn = _rownorm(y, ssm_norm_g[j], width=d_inner, tm=tm_row, name="ssd_norm")
            mix = [(o_c, c_width, 0, 0)] + [(yn, c_width, b, b + 1) for b in range(d_inner // c_width)]
            w_out, w_out_layer = cd_w_out, j

        mixed = _matmul(mix, w_out, layer=w_out_layer, ncols=d, tm=tm_mm, tn=512, out_dtype=F32, name="mix_out")
        h, xn = _addnorm(h, mixed, g[1], g[2], tm=tm_row, name="add_norm_mix")
        act = _ffn_up(xn, ffn_w_up, ffn_conv_w, ffn_conv_b, layer=layer, tm=tm_ff, tn=tn_ff, name="ffn_up")
        ff = _matmul([(act, d_ff, 0, 0)], w_down, layer=layer, ncols=d, tm=tm_mid, tn=256, out_dtype=F32,
                     name="ffn_down")
        if layer + 1 < depth:
            h, xn = _addnorm(h, ff, g[3], norm_g[layer + 1, 0], tm=tm_row, name="add_norm_ffn")

    out = _add_rows(h, ff, norm_g[depth - 1, 3], row0=N_META, nrows=seq, tm=_pick(seq, (256, 240, 128, 16, 8)),
                    name="add_out")
    return out[None]
```

```python
import functools
import math

import jax
import jax.numpy as jnp
from jax import lax
from jax.experimental import pallas as pl
from jax.experimental.pallas import tpu as pltpu

F32 = jnp.float32
BF16 = jnp.bfloat16
HIGHEST = lax.Precision.HIGHEST

CHUNK = 64
N_META = 16
Q_BLOCK = 128
NORM_EPS = 1e-6
A_DK = 128
A_DV = 128
B_NOPE = 128
B_ROPE = 64
B_DV = 128
B_Q_LORA = 1024
B_KV_LORA = 512
ROPE_THETA = 10000.0
POOL_WINDOWS = (2, 4, 8, 16)
D_HEAD_DIM = 64
D_GROUPS = 8
D_STATE = 128
D_CONV = 4
FFN_CONV = 3

SCAN_T = 64
SUB_T = 16
LANES = 128
HALO_BF16 = 16
HALO_F32 = 8
MIB = 1024 * 1024

NT_DIMS = (((1,), (1,)), ((), ()))
TN_DIMS = (((0,), (0,)), ((), ()))


def _cp(sem, vmem_mib=48):
    return pltpu.CompilerParams(dimension_semantics=sem, vmem_limit_bytes=vmem_mib * MIB)


def _pick(n, cands):
    for c in cands:
        if n % c == 0:
            return c
    raise ValueError(f"no tile for {n} in {cands}")


def _nchunks(rows, most=6):
    tiles = rows // HALO_BF16
    return max(c for c in range(1, most + 1) if tiles % c == 0)


def _rms(x, g):
    return x * lax.rsqrt(jnp.mean(x * x, axis=-1, keepdims=True) + NORM_EPS) * g


def _mm_body(*refs, n_lhs, wt):
    o_ref = refs[-1]
    acc = None
    for x_ref, w_ref in zip(refs[:n_lhs], refs[n_lhs:2 * n_lhs]):
        w = w_ref[...].astype(BF16)
        if wt:
            part = lax.dot_general(x_ref[...], w, NT_DIMS, preferred_element_type=F32)
        else:
            part = jnp.dot(x_ref[...], w, preferred_element_type=F32)
        acc = part if acc is None else acc + part
    o_ref[...] = acc.astype(o_ref.dtype)


def _matmul(pieces, w, *, layer=0, col0=0, ncols, tm, tn, out_dtype, name, wt=False, vmem_mib=56):
    rows = pieces[0][0].shape[0]
    assert col0 % tn == 0 and ncols % tn == 0 and rows % tm == 0
    cb = col0 // tn
    x_specs = [pl.BlockSpec((tm, k), lambda i, j, xb=xb: (i, xb)) for _, k, xb, _ in pieces]
    if wt:
        w_specs = [pl.BlockSpec((pl.Squeezed(), tn, k), lambda i, j, wb=wb: (layer, cb + j, wb))
                   for _, k, _, wb in pieces]
    else:
        w_specs = [pl.BlockSpec((pl.Squeezed(), k, tn), lambda i, j, wb=wb: (layer, wb, cb + j))
                   for _, k, _, wb in pieces]
    return pl.pallas_call(
        functools.partial(_mm_body, n_lhs=len(pieces), wt=wt),
        grid=(rows // tm, ncols // tn),
        in_specs=x_specs + w_specs,
        out_specs=pl.BlockSpec((tm, tn), lambda i, j: (i, j)),
        out_shape=jax.ShapeDtypeStruct((rows, ncols), out_dtype),
        compiler_params=_cp(("parallel", "parallel"), vmem_mib),
        name=name,
    )(*[p[0] for p in pieces], *([w] * len(pieces)))


def _mm_nt_body(w_ref, x_ref, o_ref):
    o_ref[...] = lax.dot_general(w_ref[...], x_ref[...], NT_DIMS,
                                 preferred_element_type=F32).astype(o_ref.dtype)


def _matmul_nt(wt, x, *, tm, tn, name):
    n, k = wt.shape
    rows = x.shape[0]
    return pl.pallas_call(
        _mm_nt_body,
        grid=(n // tn, rows // tm),
        in_specs=[pl.BlockSpec((tn, k), lambda a, b: (a, 0)),
                  pl.BlockSpec((tm, k), lambda a, b: (b, 0))],
        out_specs=pl.BlockSpec((tn, tm), lambda a, b: (a, b)),
        out_shape=jax.ShapeDtypeStruct((n, rows), BF16),
        compiler_params=_cp(("parallel", "parallel")),
        name=name,
    )(wt, x)


def _rownorm_body(x_ref, g_ref, o_ref):
    o_ref[...] = _rms(x_ref[...].astype(F32), g_ref[...]).astype(o_ref.dtype)


def _rownorm(x, g, *, width, xblk=0, tm, name):
    rows = x.shape[0]
    return pl.pallas_call(
        _rownorm_body,
        grid=(rows // tm,),
        in_specs=[pl.BlockSpec((tm, width), lambda i: (i, xblk)),
                  pl.BlockSpec((1, width), lambda i: (0, 0))],
        out_specs=pl.BlockSpec((tm, width), lambda i: (i, 0)),
        out_shape=jax.ShapeDtypeStruct((rows, width), BF16),
        compiler_params=_cp(("parallel",)),
        name=name,
    )(x, g.reshape(1, width).astype(F32))


def _addnorm_body(h_ref, m_ref, g1_ref, g2_ref, ho_ref, xo_ref):
    hn = h_ref[...] + _rms(m_ref[...].astype(F32), g1_ref[...])
    ho_ref[...] = hn
    xo_ref[...] = _rms(hn, g2_ref[...]).astype(xo_ref.dtype)


def _add_body(h_ref, m_ref, g1_ref, ho_ref):
    ho_ref[...] = h_ref[...] + _rms(m_ref[...].astype(F32), g1_ref[...])


def _add_rows(h, m, g_post, *, row0, nrows, tm, name):
    d = h.shape[1]
    assert row0 % HALO_F32 == 0 and tm % HALO_F32 == 0
    win = pl.BlockSpec((pl.Element(tm), pl.Element(d)),
                       lambda i: (pl.multiple_of(row0 + i * tm, HALO_F32), 0))
    return pl.pallas_call(
        _add_body, grid=(nrows // tm,),
        in_specs=[win, win, pl.BlockSpec((1, d), lambda i: (0, 0))],
        out_specs=pl.BlockSpec((tm, d), lambda i: (i, 0)),
        out_shape=jax.ShapeDtypeStruct((nrows, d), F32),
        compiler_params=_cp(("parallel",)), name=name,
    )(h, m, g_post.reshape(1, d))


def _addnorm(h, m, g_post, g_next, *, tm, name):
    rows, d = h.shape
    row_spec = pl.BlockSpec((tm, d), lambda i: (i, 0))
    g_spec = pl.BlockSpec((1, d), lambda i: (0, 0))
    return pl.pallas_call(
        _addnorm_body, grid=(rows // tm,),
        in_specs=[row_spec, row_spec, g_spec, g_spec], out_specs=[row_spec, row_spec],
        out_shape=[jax.ShapeDtypeStruct((rows, d), F32), jax.ShapeDtypeStruct((rows, d), BF16)],
        compiler_params=_cp(("parallel",)), name=name,
    )(h, m, g_post.reshape(1, d), g_next.reshape(1, d))


def _ffn_up_body(x_ref, wg_ref, wv_ref, cwg_ref, cwv_ref, cbg_ref, cbv_ref, o_ref, tail_ref, *, n_mm):
    j = pl.program_id(1)

    @pl.when(pl.program_id(0) == 0)
    def _():
        tail_ref[j] = jnp.zeros(tail_ref.shape[1:], F32)

    mm_rows = x_ref.shape[0] // n_mm
    wg = wg_ref[...].astype(BF16)
    wv = wv_ref[...].astype(BF16)
    cwg, cwv = cwg_ref[...], cwv_ref[...]
    cbg, cbv = cbg_ref[...], cbv_ref[...]

    def matmuls(c):
        xc = x_ref[c * mm_rows:(c + 1) * mm_rows, :]
        return (jnp.dot(xc, wg, preferred_element_type=F32), jnp.dot(xc, wv, preferred_element_type=F32))

    def conv(u, above, cw, cb):
        u = jnp.concatenate([above, u], axis=0)
        y = cw[2:3, :] * u + cw[1:2, :] * pltpu.roll(u, 1, 0) + cw[0:1, :] * pltpu.roll(u, 2, 0) + cb
        return y[HALO_F32:, :]

    tail = lambda u: (u[0][mm_rows - HALO_F32:, :], u[1][mm_rows - HALO_F32:, :])
    carried = tail_ref[j]
    above = (carried[0], carried[1])
    u_cur = matmuls(0)
    for c in range(n_mm):
        u_next = matmuls(c + 1) if c + 1 < n_mm else None
        gate = conv(u_cur[0], above[0], cwg, cbg)
        val = conv(u_cur[1], above[1], cwv, cbv)
        o_ref[c * mm_rows:(c + 1) * mm_rows, :] = (jax.nn.silu(gate) * val).astype(o_ref.dtype)
        above = tail(u_cur)
        u_cur = u_next
    tail_ref[j] = jnp.stack(above)


def _ffn_up(xn, w_up, conv_w, conv_b, *, layer, tm, tn, name):
    rows, d = xn.shape
    d_ff = w_up.shape[2] // 2
    nj = d_ff // tn
    cb = conv_b.reshape(conv_b.shape[0], 1, 2 * d_ff)
    sq = pl.Squeezed()
    return pl.pallas_call(
        functools.partial(_ffn_up_body, n_mm=_nchunks(tm, most=8)),
        grid=(rows // tm, nj),
        in_specs=[pl.BlockSpec((tm, d), lambda i, j: (i, 0)),
                  pl.BlockSpec((sq, d, tn), lambda i, j: (layer, 0, j)),
                  pl.BlockSpec((sq, d, tn), lambda i, j: (layer, 0, nj + j)),
                  pl.BlockSpec((sq, FFN_CONV, tn), lambda i, j: (layer, 0, j)),
                  pl.BlockSpec((sq, FFN_CONV, tn), lambda i, j: (layer, 0, nj + j)),
                  pl.BlockSpec((sq, 1, tn), lambda i, j: (layer, 0, j)),
                  pl.BlockSpec((sq, 1, tn), lambda i, j: (layer, 0, nj + j))],
        out_specs=pl.BlockSpec((tm, tn), lambda i, j: (i, j)),
        out_shape=jax.ShapeDtypeStruct((rows, d_ff), BF16),
        scratch_shapes=[pltpu.VMEM((nj, 2, HALO_F32, tn), F32)],
        compiler_params=_cp(("arbitrary", "arbitrary"), 56),
        name=name,
    )(xn, w_up, w_up, conv_w, conv_w, cb, cb)


def _cumsum_chunks(x):
    row = lax.broadcasted_iota(jnp.int32, x.shape, 0) % SCAN_T
    sh = 1
    while sh < SCAN_T:
        x = x + jnp.where(row >= sh, pltpu.roll(x, sh, 0), 0.0)
        sh *= 2
    return x


def _chunk_rows(rows, reps):
    return jnp.concatenate([jnp.broadcast_to(r, (reps, r.shape[1])) for r in rows], axis=0)


def _hgrn2_body(q_ref, f_ref, v_ref, g_ref, lb_ref, hg_ref, o_ref, st_ref, *, nsub):
    @pl.when(pl.program_id(1) == 0)
    def _():
        st_ref[...] = jnp.zeros_like(st_ref)

    t = SCAN_T
    nblk = t // SUB_T
    lb = lb_ref[...]
    ri = lax.broadcasted_iota(jnp.int32, (t, t), 0)
    ci = lax.broadcasted_iota(jnp.int32, (t, t), 1)
    tril = ci <= ri

    q = q_ref[...]
    fp = f_ref[...]
    vb = v_ref[...].astype(BF16)
    row = lax.broadcasted_iota(jnp.int32, q.shape, 0) % t
    lf = jnp.log(lb + (1.0 - lb) * jax.nn.sigmoid(fp))
    k = (1.0 - lb) * jax.nn.sigmoid(-fp)
    b = _cumsum_chunks(lf)
    ref = [[b[c * t + u * SUB_T + SUB_T // 2 - 1:c * t + u * SUB_T + SUB_T // 2, :] for u in range(nblk)]
           for c in range(nsub)]
    last = [b[c * t + t - 1:c * t + t, :] for c in range(nsub)]
    r_own = _chunk_rows([ref[c][u] for c in range(nsub) for u in range(nblk)], SUB_T)
    qt = (q * jnp.exp(b - r_own)).astype(BF16)
    kh = [jnp.where(row < (u + 1) * SUB_T,
                    k * jnp.exp(_chunk_rows([ref[c][u] for c in range(nsub)], t) - b), 0.0).astype(BF16)
          for u in range(nblk)]
    kd = (k * jnp.exp(_chunk_rows(last, t) - b)).astype(BF16)
    qe = (q * jnp.exp(b)).astype(BF16)

    intra, upd = [], []
    for c in range(nsub):
        lo = c * t
        rows = [lax.dot_general(qt[lo + u * SUB_T:lo + (u + 1) * SUB_T, :], kh[u][lo:lo + t, :], NT_DIMS,
                                preferred_element_type=F32) for u in range(nblk)]
        a = jnp.where(tril, jnp.concatenate(rows, axis=0), 0.0).astype(BF16)
        intra.append(jnp.dot(a, vb[lo:lo + t, :], preferred_element_type=F32))
        upd.append(lax.dot_general(vb[lo:lo + t, :], kd[lo:lo + t, :], TN_DIMS, preferred_element_type=F32))

    st = st_ref[...]
    outs = []
    for c in range(nsub):
        outs.append(intra[c] + lax.dot_general(qe[c * t:(c + 1) * t, :], st.astype(BF16), NT_DIMS,
                                               preferred_element_type=F32))
        st = st * jnp.exp(last[c]) + upd[c]
    st_ref[...] = st
    o = _rms(jnp.concatenate(outs, axis=0), hg_ref[...])
    o_ref[...] = (o * jax.nn.silu(g_ref[...])).astype(o_ref.dtype)


def _hgrn2(proj, lb, hg, *, heads, tb, name):
    rows = proj.shape[0]
    blk = lambda off: pl.BlockSpec((tb, A_DK), lambda h, c, off=off: (c, off + h))
    vec = pl.BlockSpec((pl.Squeezed(), 1, A_DK), lambda h, c: (h, 0, 0))
    return pl.pallas_call(
        functools.partial(_hgrn2_body, nsub=tb // SCAN_T),
        grid=(heads, rows // tb),
        in_specs=[blk(0), blk(heads), blk(2 * heads), blk(3 * heads), vec, vec],
        out_specs=pl.BlockSpec((tb, A_DV), lambda h, c: (c, h)),
        out_shape=jax.ShapeDtypeStruct((rows, heads * A_DV), BF16),
        scratch_shapes=[pltpu.VMEM((A_DV, A_DK), F32)],
        compiler_params=_cp(("parallel", "arbitrary")),
        name=name,
    )(proj, proj, proj, proj, lb.reshape(heads, 1, A_DK), hg.reshape(heads, 1, A_DV))


def _kprep_body(kr_ref, cs_ref, o_ref):
    t = kr_ref[...] * cs_ref[...]
    o_ref[...] = (t + pltpu.roll(t, B_ROPE, 1)).astype(o_ref.dtype)


def _kprep(krp, cs, *, tm, name):
    rows = krp.shape[0]
    spec = pl.BlockSpec((tm, LANES), lambda i: (i, 0))
    return pl.pallas_call(
        _kprep_body, grid=(rows // tm,), in_specs=[spec, spec], out_specs=spec,
        out_shape=jax.ShapeDtypeStruct((rows, LANES), BF16),
        compiler_params=_cp(("parallel",)), name=name,
    )(krp, cs)


def _attn_body(qn_ref, qp_ref, cs_ref, kn_ref, kp_ref, vt_ref, o_ref, kcat_ref, *, tq, tk, n_kv, scale, hpb):
    i = pl.program_id(1)

    hs = [slice(h * LANES, (h + 1) * LANES) for h in range(hpb)]

    @pl.when(i == 0)
    def _():
        for h in range(hpb):
            kcat_ref[h, :, 0:B_NOPE] = kn_ref[:, hs[h]]
            kcat_ref[h, :, B_NOPE:] = kp_ref[...]

    qs = i * tq
    cs = cs_ref[...]
    qcat = [jnp.concatenate([(qn_ref[:, hs[h]] * scale).astype(BF16),
                             (qp_ref[:, hs[h]] * cs * scale).astype(BF16)], axis=1) for h in range(hpb)]
    off = CHUNK - N_META
    first_cid = (qs + off) // CHUNK
    last_cid = (qs + tq - 1 + off) // CHUNK
    n_full = jnp.minimum((CHUNK * first_cid + N_META) // tk, n_kv)
    n_vis = jnp.minimum((CHUNK * last_cid + N_META - 1) // tk + 1, n_kv)
    q_cid = (qs + lax.broadcasted_iota(jnp.int32, (tk, tq), 1) + off) // CHUNK
    k_iota = lax.broadcasted_iota(jnp.int32, (tk, tq), 0) + off

    def make_step(masked):
        def step(j, carry):
            ks = pl.multiple_of(j * tk, tk)
            s = [lax.dot_general(kcat_ref[h, pl.ds(ks, tk), :], qcat[h], NT_DIMS, preferred_element_type=F32)
                 for h in range(hpb)]
            out = []
            for h in range(hpb):
                m, l, acc = carry[h]
                sh = jnp.where((k_iota + ks) // CHUNK <= q_cid, s[h], -jnp.inf) if masked else s[h]
                m_new = jnp.maximum(m, jnp.max(sh, axis=0, keepdims=True))
                p = jnp.exp2(sh - m_new)
                alpha = jnp.exp2(m - m_new)
                l = alpha * l + jnp.sum(p, axis=0, keepdims=True)
                acc = alpha * acc + jnp.dot(vt_ref[hs[h], pl.ds(ks, tk)], p.astype(BF16),
                                            preferred_element_type=F32)
                out.append((m_new, l, acc))
            return tuple(out)
        return step

    init = tuple((jnp.full((1, tq), -jnp.inf, F32), jnp.zeros((1, tq), F32), jnp.zeros((B_DV, tq), F32))
                 for _ in range(hpb))
    plain = make_step(False)
    pairs = n_full // 2
    carry = lax.fori_loop(0, pairs, lambda jj, c: plain(2 * jj + 1, plain(2 * jj, c)), init)
    carry = lax.fori_loop(2 * pairs, n_full, plain, carry)
    carry = lax.fori_loop(n_full, n_vis, make_step(True), carry)
    for h in range(hpb):
        _, l, acc = carry[h]
        o_ref[:, hs[h]] = (acc / l).T.astype(o_ref.dtype)


def _attention(qf, cs, kn, kp, vt, *, heads, tq, tk, name, hpb=2):
    rows = qf.shape[0]
    scale = float(B_NOPE + B_ROPE) ** -0.5 * math.log2(math.e)
    wide = hpb * LANES
    groups = heads // hpb
    return pl.pallas_call(
        functools.partial(_attn_body, tq=tq, tk=tk, n_kv=rows // tk, scale=scale, hpb=hpb),
        grid=(groups, rows // tq),
        in_specs=[pl.BlockSpec((tq, wide), lambda h, i: (i, h)),
                  pl.BlockSpec((tq, wide), lambda h, i: (i, groups + h)),
                  pl.BlockSpec((tq, LANES), lambda h, i: (i, 0)),
                  pl.BlockSpec((rows, wide), lambda h, i: (0, h)),
                  pl.BlockSpec((rows, LANES), lambda h, i: (0, 0)),
                  pl.BlockSpec((wide, rows), lambda h, i: (h, 0))],
        out_specs=pl.BlockSpec((tq, wide), lambda h, i: (i, h)),
        out_shape=jax.ShapeDtypeStruct((rows, heads * B_DV), BF16),
        scratch_shapes=[pltpu.VMEM((hpb, rows, B_NOPE + LANES), BF16)],
        compiler_params=_cp(("parallel", "arbitrary")),
        name=name,
    )(qf, qf, cs, kn, kp, vt)


def _pool_body(u_ref, halo_ref, w_ref, sc_ref, o_ref, *, tm, gdim):
    i = pl.program_id(0)
    hl = halo_ref[...]
    hl = jnp.where(i == 0, jnp.zeros_like(hl), hl)
    halo = hl.shape[0]
    pos = i * tm + lax.broadcasted_iota(jnp.int32, (tm, gdim), 0)
    for gi, win in enumerate(POOL_WINDOWS):
        cols = slice(gi * gdim, (gi + 1) * gdim)
        x = u_ref[:, cols]
        s = jnp.concatenate([hl[:, cols], x], axis=0)
        sh = 1
        while sh < win:
            s = s + pltpu.roll(s, sh, 0)
            sh *= 2
        cnt = jnp.minimum(pos + 1, win).astype(F32)
        pooled = s[halo:, :] / cnt - x
        y = jnp.dot(pooled.astype(BF16), w_ref[gi], preferred_element_type=F32)
        o_ref[:, cols] = (y * sc_ref[:, cols]).astype(o_ref.dtype)


def _pool(u, pool_w, pool_scale, *, tm, name):
    rows = u.shape[0]
    groups, gdim, _ = pool_w.shape
    cw = groups * gdim
    halo = 2 * HALO_F32
    hb = tm // halo
    return pl.pallas_call(
        functools.partial(_pool_body, tm=tm, gdim=gdim),
        grid=(rows // tm,),
        in_specs=[pl.BlockSpec((tm, cw), lambda i: (i, 0)),
                  pl.BlockSpec((halo, cw), lambda i: (jnp.maximum(i * hb - 1, 0), 0)),
                  pl.BlockSpec((groups, gdim, gdim), lambda i: (0, 0, 0)),
                  pl.BlockSpec((1, cw), lambda i: (0, 0))],
        out_specs=pl.BlockSpec((tm, cw), lambda i: (i, 0)),
        out_shape=jax.ShapeDtypeStruct((rows, cw), BF16),
        compiler_params=_cp(("parallel",)),
        name=name,
    )(u, u, pool_w.astype(BF16), pool_scale.reshape(1, cw))


def _split3_dot(x, sel):
    hi = x.astype(BF16)
    r1 = x - hi.astype(F32)
    mid = r1.astype(BF16)
    lo = (r1 - mid.astype(F32)).astype(BF16)
    return (jnp.dot(hi, sel, preferred_element_type=F32) + jnp.dot(mid, sel, preferred_element_type=F32)
            + jnp.dot(lo, sel, preferred_element_type=F32))


def _ssd_body(xs_ref, xh_ref, b_ref, bh_ref, c_ref, ch_ref, dt_ref, z_ref, cwx_ref, cwb_ref, cwc_ref,
              cbx_ref, cbb_ref, cbc_ref, dtb_ref, ah_ref, dsk_ref, o_ref, st_ref, *, nsub, hpg):
    g = pl.program_id(0)
    first = pl.program_id(1) == 0

    @pl.when(first)
    def _():
        st_ref[...] = jnp.zeros_like(st_ref)

    t = SCAN_T
    gw = hpg * D_HEAD_DIM

    def conv(x_ref, halo_ref, cw_ref, cb_ref):
        hl = halo_ref[...]
        x = jnp.concatenate([jnp.where(first, jnp.zeros_like(hl), hl), x_ref[...]], axis=0)
        cw = cw_ref[...]
        y = cw[D_CONV - 1:D_CONV, :] * x + cb_ref[...]
        for kk in range(1, D_CONV):
            y = y + cw[D_CONV - 1 - kk:D_CONV - kk, :] * pltpu.roll(x, kk, 0)
        return jax.nn.silu(y[HALO_F32:, :])

    xs = conv(xs_ref, xh_ref, cwx_ref, cbx_ref)
    bb = conv(b_ref, bh_ref, cwb_ref, cbb_ref).astype(BF16)
    cc = conv(c_ref, ch_ref, cwc_ref, cbc_ref).astype(BF16)
    dt = jax.nn.softplus(dt_ref[...] + dtb_ref[...])
    acs_h = _cumsum_chunks(dt * ah_ref[...])
    hrow = lax.broadcasted_iota(jnp.int32, (LANES, gw), 0)
    hcol = lax.broadcasted_iota(jnp.int32, (LANES, gw), 1) // D_HEAD_DIM + g * hpg
    sel = (hrow == hcol).astype(BF16)
    dtx = _split3_dot(dt, sel)
    acs = _split3_dot(acs_h, sel)
    xd = xs * dtx
    decay_in = jnp.exp(acs)
    last = [acs[c * t + t - 1:c * t + t, :] for c in range(nsub)]
    xdd = (xd * jnp.exp(_chunk_rows(last, t) - acs)).astype(BF16)

    pr = lax.broadcasted_iota(jnp.int32, (t, 2 * t), 0)
    pc = lax.broadcasted_iota(jnp.int32, (t, 2 * t), 1)
    pj = pc % t
    tril2 = pj <= pr
    eye2f = (pr == pj).astype(F32)
    left = pc < t

    diag, upd = [], []
    for c in range(nsub):
        rs = slice(c * t, (c + 1) * t)
        bc = bb[rs, :]
        cb2 = lax.dot_general(cc[rs, :], jnp.concatenate([bc, bc], axis=0), NT_DIMS, preferred_element_type=F32)
        parts = []
        for p in range(hpg // 2):
            cols = slice(p * 2 * D_HEAD_DIM, (p + 1) * 2 * D_HEAD_DIM)
            ac = acs[rs, cols]
            row_cs = jnp.sum(ac * eye2f, axis=0, keepdims=True)
            lmat = jnp.where(tril2, jnp.exp(ac - row_cs), 0.0)
            xdp = xd[rs, cols]
            rhs = jnp.concatenate([jnp.where(left, xdp, 0.0), jnp.where(left, 0.0, xdp)], axis=0)
            parts.append(jnp.dot((cb2 * lmat).astype(BF16), rhs.astype(BF16), preferred_element_type=F32))
        diag.append(jnp.concatenate(parts, axis=1))
        upd.append(lax.dot_general(bc, xdd[rs, :], TN_DIMS, preferred_element_type=F32))

    st = st_ref[...]
    ys = []
    for c in range(nsub):
        rs = slice(c * t, (c + 1) * t)
        ys.append(diag[c] + decay_in[rs, :] * jnp.dot(cc[rs, :], st.astype(BF16), preferred_element_type=F32))
        st = st * jnp.exp(last[c]) + upd[c]
    st_ref[...] = st
    y = jnp.concatenate(ys, axis=0) + xs * dsk_ref[...]
    o_ref[...] = y * jax.nn.silu(z_ref[...])


def _ssd(xbc, dt, z, conv_w, conv_b, dt_bias, a_h, d_ch, *, heads, tb, name):
    rows, ch = xbc.shape
    inner = heads * D_HEAD_DIM
    hpg = heads // D_GROUPS
    gw = hpg * D_HEAD_DIM
    assert hpg % 2 == 0 and 2 * D_HEAD_DIM == LANES and SCAN_T == D_HEAD_DIM
    nxb = inner // LANES
    hb = tb // HALO_F32
    prev = lambda c: jnp.maximum(c * hb - 1, 0)
    col_x = lambda g: g
    col_b = lambda g: nxb + g
    col_c = lambda g: nxb + D_GROUPS + g

    def data(width, col):
        return [pl.BlockSpec((tb, width), lambda g, c: (c, col(g))),
                pl.BlockSpec((HALO_F32, width), lambda g, c: (prev(c), col(g)))]

    def per_col(nrows, width, col):
        return pl.BlockSpec((nrows, width), lambda g, c: (0, col(g)))

    cb = conv_b.reshape(1, ch)
    return pl.pallas_call(
        functools.partial(_ssd_body, nsub=tb // SCAN_T, hpg=hpg),
        grid=(D_GROUPS, rows // tb),
        in_specs=(data(gw, col_x) + data(D_STATE, col_b) + data(D_STATE, col_c)
                  + [pl.BlockSpec((tb, LANES), lambda g, c: (c, 0)),
                     pl.BlockSpec((tb, gw), lambda g, c: (c, g)),
                     per_col(D_CONV, gw, col_x), per_col(D_CONV, D_STATE, col_b), per_col(D_CONV, D_STATE, col_c),
                     per_col(1, gw, col_x), per_col(1, D_STATE, col_b), per_col(1, D_STATE, col_c),
                     pl.BlockSpec((1, LANES), lambda g, c: (0, 0)),
                     pl.BlockSpec((1, LANES), lambda g, c: (0, 0)),
                     per_col(1, gw, col_x)]),
        out_specs=pl.BlockSpec((tb, gw), lambda g, c: (c, g)),
        out_shape=jax.ShapeDtypeStruct((rows, inner), F32),
        scratch_shapes=[pltpu.VMEM((D_STATE, gw), F32)],
        compiler_params=_cp(("parallel", "arbitrary")),
        name=name,
    )(xbc, xbc, xbc, xbc, xbc, xbc, dt, z, conv_w, conv_w, conv_w, cb, cb, cb, dt_bias, a_h, d_ch)


def _with_rot_half(w):
    half = B_ROPE // 2
    return jnp.concatenate([w, -w[..., half:], w[..., :half]], axis=-1)


def _layout_w_uq(w, heads):
    k = w.shape[0]
    w3 = w.astype(BF16).reshape(k, heads, B_NOPE + B_ROPE)
    nope = w3[:, :, :B_NOPE].reshape(k, heads * B_NOPE)
    pe = _with_rot_half(w3[:, :, B_NOPE:]).reshape(k, heads * LANES)
    return jnp.concatenate([nope, pe], axis=1)[None]


def kernel(x, meta_tokens, lb_logits, norm_g, ab_w_in, hgrn_norm_g, mla_q_norm_g, mla_kv_norm_g,
           mla_w_uq, mla_w_ukv, ab_w_out, cd_w_in, pool_w, pool_scale, ssm_conv_w, ssm_conv_b,
           ssm_dt_bias, ssm_a_log, ssm_d, ssm_norm_g, cd_w_out, ffn_w_up, ffn_conv_w, ffn_conv_b,
           ffn_w_down):
    bsz, seq, d = x.shape
    assert bsz == 1
    depth = norm_g.shape[0]
    total = N_META + seq
    lp = -(-total // Q_BLOCK) * Q_BLOCK
    a_heads = hgrn_norm_g.shape[1]
    a_kw = a_heads * A_DK
    a_w = a_heads * A_DV
    b_heads = mla_w_uq.shape[2] // (B_NOPE + B_ROPE)
    c_width = pool_scale.shape[1]
    d_inner = ssm_norm_g.shape[1]
    d_heads = ssm_dt_bias.shape[1]
    d_xbc = ssm_conv_w.shape[2]
    d_ff = ffn_w_down.shape[1]

    tm_mm = _pick(lp, (1040, 640, 512, 256, 128))
    tm_mid = _pick(lp, (640, 512, 256, 128))
    tm_row = _pick(lp, (320, 256, 128))
    tb = _pick(lp, (640, 320, 256, 128, 64))
    tq = _pick(lp, (640, 512, 256, 128))
    tn_ff = _pick(d_ff, (256, 128))
    tm_ff = _pick(lp, (1664, 1040, 640, 512, 256, 128))

    h = jnp.concatenate([meta_tokens.astype(x.dtype), x[0],
                         jnp.zeros((lp - total, d), x.dtype)], axis=0)

    inv = ROPE_THETA ** (-jnp.arange(0, B_ROPE, 2, dtype=F32) / B_ROPE)
    ang = jnp.arange(lp, dtype=F32)[:, None] * inv[None, :]
    cos, sin = jnp.cos(ang), jnp.sin(ang)
    cs = jnp.concatenate([cos, cos, sin, sin], axis=-1)
    lb_all = jnp.cumsum(jax.nn.softmax(lb_logits.astype(F32), axis=0), axis=0)
    w_up = ffn_w_up.astype(BF16)
    w_down = ffn_w_down.astype(BF16)

    xn = _rownorm(h, norm_g[0, 0], width=d, tm=tm_row, name="norm_in")

    for layer in range(depth):
        g = norm_g[layer]
        j = layer // 2
        xp = [(xn, d, 0, 0)]
        if layer % 2 == 0:
            n_ab = 2 * a_kw + 2 * a_w
            n_lora = B_Q_LORA + B_KV_LORA
            w_t = jnp.swapaxes(ab_w_in, 1, 2)
            proj_a = _matmul(xp, w_t, layer=j, col0=0, ncols=n_ab, tm=tm_mm, tn=512, out_dtype=F32, wt=True,
                             name="ab_in_a")
            proj_c = _matmul(xp, w_t, layer=j, col0=n_ab, ncols=n_lora, tm=tm_mm, tn=512, out_dtype=F32, wt=True,
                             name="ab_in_c")
            w_kr = _with_rot_half(w_t[j, n_ab + n_lora:, :].astype(BF16).T).T[None]
            krp = _matmul(xp, w_kr, ncols=LANES, tm=tm_mm, tn=LANES, out_dtype=F32, wt=True, name="ab_in_kr")
            o_a = _hgrn2(proj_a, lb_all[layer], hgrn_norm_g[j], heads=a_heads, tb=tb, name="hgrn2")
            cq = _rownorm(proj_c, mla_q_norm_g[j], width=B_Q_LORA, xblk=0, tm=tm_row, name="q_norm")
            ckv = _rownorm(proj_c, mla_kv_norm_g[j], width=B_KV_LORA, xblk=B_Q_LORA // B_KV_LORA, tm=tm_row,
                           name="kv_norm")
            w_uq = _layout_w_uq(mla_w_uq[j], b_heads)
            qf = _matmul([(cq, B_Q_LORA, 0, 0)], w_uq, ncols=w_uq.shape[2], tm=tm_mm, tn=512, out_dtype=F32,
                         name="uq")
            w_kv = mla_w_ukv[j].astype(BF16).reshape(B_KV_LORA, b_heads, 2, B_NOPE)
            w_k = w_kv[:, :, 0, :].reshape(1, B_KV_LORA, b_heads * B_NOPE)
            w_vt = w_kv[:, :, 1, :].reshape(B_KV_LORA, b_heads * B_DV).T
            kn = _matmul([(ckv, B_KV_LORA, 0, 0)], w_k, ncols=b_heads * B_NOPE, tm=tm_mm, tn=512,
                         out_dtype=BF16, name="uk")
            vt = _matmul_nt(w_vt, ckv, tm=tm_mid, tn=512, name="uv_t")
            kp = _kprep(krp, cs, tm=tm_row, name="k_rope")
            o_b = _attention(qf, cs, kn, kp, vt, heads=b_heads, tq=tq, tk=tq, name="attention")
            half = a_w
            mix = [(o_a, half, 0, 0), (o_b, half, 0, 1)]
            w_out, w_out_layer = ab_w_out, j
        else:
            n_zu = d_inner + c_width
            w_t = jnp.swapaxes(cd_w_in, 1, 2)
            u_c = _matmul(xp, w_t, layer=j, col0=0, ncols=c_width, tm=tm_mm, tn=512, out_dtype=F32, wt=True,
                          name="cd_in_u")
            z = _matmul(xp, w_t, layer=j, col0=c_width, ncols=d_inner, tm=tm_mm, tn=512, out_dtype=F32, wt=True,
                        name="cd_in_z")
            xbc = _matmul(xp, w_t, layer=j, col0=n_zu, ncols=d_xbc, tm=tm_mm, tn=512, out_dtype=F32, wt=True,
                          name="cd_in_xbc")
            dt_pad = LANES - d_heads
            w_dt = jnp.pad(w_t[j, n_zu + d_xbc:, :].astype(BF16), ((0, dt_pad), (0, 0)))[None]
            dt = _matmul(xp, w_dt, ncols=LANES, tm=tm_mm, tn=LANES, out_dtype=F32, wt=True, name="cd_in_dt")
            o_c = _pool(u_c, pool_w[j], pool_scale[j], tm=tm_row, name="pool")
            a_h = jnp.pad(-jnp.exp(ssm_a_log[j].astype(F32)), (0, dt_pad)).reshape(1, LANES)
            d_ch = jnp.repeat(ssm_d[j].astype(F32), D_HEAD_DIM).reshape(1, d_inner)
            dtb = jnp.pad(ssm_dt_bias[j].astype(F32), (0, dt_pad)).reshape(1, LANES)
            y = _ssd(xbc, dt, z, ssm_conv_w[j], ssm_conv_b[j], dtb, a_h, d_ch, heads=d_heads, tb=tb, name="ssd")
            yn = _rownorm(y, ssm_norm_g[j], width=d_inner, tm=tm_row, name="ssd_norm")
            mix = [(o_c, c_width, 0, 0)] + [(yn, c_width, b, b + 1) for b in range(d_inner // c_width)]
            w_out, w_out_layer = cd_w_out, j

        mixed = _matmul(mix, w_out, layer=w_out_layer, ncols=d, tm=tm_mm, tn=512, out_dtype=BF16, name="mix_out")
        h, xn = _addnorm(h, mixed, g[1], g[2], tm=tm_row, name="add_norm_mix")
        act = _ffn_up(xn, ffn_w_up, ffn_conv_w, ffn_conv_b, layer=layer, tm=tm_ff, tn=tn_ff, name="ffn_up")
        ff = _matmul([(act, d_ff, 0, 0)], w_down, layer=layer, ncols=d, tm=tm_mid, tn=256, out_dtype=BF16,
                     name="ffn_down")
        if layer + 1 < depth:
            h, xn = _addnorm(h, ff, g[3], norm_g[layer + 1, 0], tm=tm_row, name="add_norm_ffn")

    out = _add_rows(h, ff, norm_g[depth - 1, 3], row0=N_META, nrows=seq, tm=_pick(seq, (256, 240, 128, 16, 8)),
                    name="add_out")
    return out[None]
```

```python
import functools
import math

import jax
import jax.numpy as jnp
from jax import lax
from jax.experimental import pallas as pl
from jax.experimental.pallas import tpu as pltpu

F32 = jnp.float32
BF16 = jnp.bfloat16
HIGHEST = lax.Precision.HIGHEST

CHUNK = 64
N_META = 16
Q_BLOCK = 128
NORM_EPS = 1e-6
A_DK = 128
A_DV = 128
B_NOPE = 128
B_ROPE = 64
B_DV = 128
B_Q_LORA = 1024
B_KV_LORA = 512
ROPE_THETA = 10000.0
POOL_WINDOWS = (2, 4, 8, 16)
D_HEAD_DIM = 64
D_GROUPS = 8
D_STATE = 128
D_CONV = 4
FFN_CONV = 3

SCAN_T = 64
SUB_T = 16
LANES = 128
HALO_BF16 = 16
HALO_F32 = 8
MIB = 1024 * 1024

NT_DIMS = (((1,), (1,)), ((), ()))
TN_DIMS = (((0,), (0,)), ((), ()))


def _cp(sem, vmem_mib=48):
    return pltpu.CompilerParams(dimension_semantics=sem, vmem_limit_bytes=vmem_mib * MIB)


def _pick(n, cands):
    for c in cands:
        if n % c == 0:
            return c
    raise ValueError(f"no tile for {n} in {cands}")


def _nchunks(rows, most=6):
    tiles = rows // HALO_BF16
    return max(c for c in range(1, most + 1) if tiles % c == 0)


def _rms(x, g):
    return x * lax.rsqrt(jnp.mean(x * x, axis=-1, keepdims=True) + NORM_EPS) * g


def _mm_body(*refs, n_lhs, wt):
    o_ref = refs[-1]
    acc = None
    for x_ref, w_ref in zip(refs[:n_lhs], refs[n_lhs:2 * n_lhs]):
        w = w_ref[...].astype(BF16)
        if wt:
            part = lax.dot_general(x_ref[...], w, NT_DIMS, preferred_element_type=F32)
        else:
            part = jnp.dot(x_ref[...], w, preferred_element_type=F32)
        acc = part if acc is None else acc + part
    o_ref[...] = acc.astype(o_ref.dtype)


def _matmul(pieces, w, *, layer=0, col0=0, ncols, tm, tn, out_dtype, name, wt=False, vmem_mib=56):
    rows = pieces[0][0].shape[0]
    assert col0 % tn == 0 and ncols % tn == 0 and rows % tm == 0
    cb = col0 // tn
    x_specs = [pl.BlockSpec((tm, k), lambda i, j, xb=xb: (i, xb)) for _, k, xb, _ in pieces]
    if wt:
        w_specs = [pl.BlockSpec((pl.Squeezed(), tn, k), lambda i, j, wb=wb: (layer, cb + j, wb))
                   for _, k, _, wb in pieces]
    else:
        w_specs = [pl.BlockSpec((pl.Squeezed(), k, tn), lambda i, j, wb=wb: (layer, wb, cb + j))
                   for _, k, _, wb in pieces]
    return pl.pallas_call(
        functools.partial(_mm_body, n_lhs=len(pieces), wt=wt),
        grid=(rows // tm, ncols // tn),
        in_specs=x_specs + w_specs,
        out_specs=pl.BlockSpec((tm, tn), lambda i, j: (i, j)),
        out_shape=jax.ShapeDtypeStruct((rows, ncols), out_dtype),
        compiler_params=_cp(("parallel", "parallel"), vmem_mib),
        name=name,
    )(*[p[0] for p in pieces], *([w] * len(pieces)))


def _mm_nt_body(w_ref, x_ref, o_ref):
    o_ref[...] = lax.dot_general(w_ref[...], x_ref[...], NT_DIMS,
                                 preferred_element_type=F32).astype(o_ref.dtype)


def _matmul_nt(wt, x, *, tm, tn, name):
    n, k = wt.shape
    rows = x.shape[0]
    return pl.pallas_call(
        _mm_nt_body,
        grid=(n // tn, rows // tm),
        in_specs=[pl.BlockSpec((tn, k), lambda a, b: (a, 0)),
                  pl.BlockSpec((tm, k), lambda a, b: (b, 0))],
        out_specs=pl.BlockSpec((tn, tm), lambda a, b: (a, b)),
        out_shape=jax.ShapeDtypeStruct((n, rows), BF16),
        compiler_params=_cp(("parallel", "parallel")),
        name=name,
    )(wt, x)


def _rownorm_body(x_ref, g_ref, o_ref):
    o_ref[...] = _rms(x_ref[...].astype(F32), g_ref[...]).astype(o_ref.dtype)


def _rownorm(x, g, *, width, xblk=0, tm, name):
    rows = x.shape[0]
    return pl.pallas_call(
        _rownorm_body,
        grid=(rows // tm,),
        in_specs=[pl.BlockSpec((tm, width), lambda i: (i, xblk)),
                  pl.BlockSpec((1, width), lambda i: (0, 0))],
        out_specs=pl.BlockSpec((tm, width), lambda i: (i, 0)),
        out_shape=jax.ShapeDtypeStruct((rows, width), BF16),
        compiler_params=_cp(("parallel",)),
        name=name,
    )(x, g.reshape(1, width).astype(F32))


def _addnorm_body(h_ref, m_ref, g1_ref, g2_ref, ho_ref, xo_ref):
    hn = h_ref[...] + _rms(m_ref[...].astype(F32), g1_ref[...])
    ho_ref[...] = hn
    xo_ref[...] = _rms(hn, g2_ref[...]).astype(xo_ref.dtype)


def _add_body(h_ref, m_ref, g1_ref, ho_ref):
    ho_ref[...] = h_ref[...] + _rms(m_ref[...].astype(F32), g1_ref[...])


def _add_rows(h, m, g_post, *, row0, nrows, tm, name):
    d = h.shape[1]
    assert row0 % HALO_F32 == 0 and tm % HALO_F32 == 0
    win = pl.BlockSpec((pl.Element(tm), pl.Element(d)),
                       lambda i: (pl.multiple_of(row0 + i * tm, HALO_F32), 0))
    return pl.pallas_call(
        _add_body, grid=(nrows // tm,),
        in_specs=[win, win, pl.BlockSpec((1, d), lambda i: (0, 0))],
        out_specs=pl.BlockSpec((tm, d), lambda i: (i, 0)),
        out_shape=jax.ShapeDtypeStruct((nrows, d), F32),
        compiler_params=_cp(("parallel",)), name=name,
    )(h, m, g_post.reshape(1, d))


def _addnorm(h, m, g_post, g_next, *, tm, name):
    rows, d = h.shape
    row_spec = pl.BlockSpec((tm, d), lambda i: (i, 0))
    g_spec = pl.BlockSpec((1, d), lambda i: (0, 0))
    return pl.pallas_call(
        _addnorm_body, grid=(rows // tm,),
        in_specs=[row_spec, row_spec, g_spec, g_spec], out_specs=[row_spec, row_spec],
        out_shape=[jax.ShapeDtypeStruct((rows, d), F32), jax.ShapeDtypeStruct((rows, d), BF16)],
        compiler_params=_cp(("parallel",)), name=name,
    )(h, m, g_post.reshape(1, d), g_next.reshape(1, d))


def _ffn_up_body(x_ref, wg_ref, wv_ref, cwg_ref, cwv_ref, cbg_ref, cbv_ref, o_ref, tail_ref, *, n_mm):
    j = pl.program_id(1)

    @pl.when(pl.program_id(0) == 0)
    def _():
        tail_ref[j] = jnp.zeros(tail_ref.shape[1:], F32)

    mm_rows = x_ref.shape[0] // n_mm
    wg = wg_ref[...].astype(BF16)
    wv = wv_ref[...].astype(BF16)
    cwg, cwv = cwg_ref[...], cwv_ref[...]
    cbg, cbv = cbg_ref[...], cbv_ref[...]

    def matmuls(c):
        xc = x_ref[c * mm_rows:(c + 1) * mm_rows, :]
        return (jnp.dot(xc, wg, preferred_element_type=F32), jnp.dot(xc, wv, preferred_element_type=F32))

    def conv(u, above, cw, cb):
        u = jnp.concatenate([above, u], axis=0)
        y = cw[2:3, :] * u + cw[1:2, :] * pltpu.roll(u, 1, 0) + cw[0:1, :] * pltpu.roll(u, 2, 0) + cb
        return y[HALO_F32:, :]

    tail = lambda u: (u[0][mm_rows - HALO_F32:, :], u[1][mm_rows - HALO_F32:, :])
    carried = tail_ref[j]
    above = (carried[0], carried[1])
    u_cur = matmuls(0)
    for c in range(n_mm):
        u_next = matmuls(c + 1) if c + 1 < n_mm else None
        gate = conv(u_cur[0], above[0], cwg, cbg)
        val = conv(u_cur[1], above[1], cwv, cbv)
        o_ref[c * mm_rows:(c + 1) * mm_rows, :] = (jax.nn.silu(gate) * val).astype(o_ref.dtype)
        above = tail(u_cur)
        u_cur = u_next
    tail_ref[j] = jnp.stack(above)


def _ffn_up(xn, w_up, conv_w, conv_b, *, layer, tm, tn, name):
    rows, d = xn.shape
    d_ff = w_up.shape[2] // 2
    nj = d_ff // tn
    cb = conv_b.reshape(conv_b.shape[0], 1, 2 * d_ff)
    sq = pl.Squeezed()
    return pl.pallas_call(
        functools.partial(_ffn_up_body, n_mm=_nchunks(tm, most=8)),
        grid=(rows // tm, nj),
        in_specs=[pl.BlockSpec((tm, d), lambda i, j: (i, 0)),
                  pl.BlockSpec((sq, d, tn), lambda i, j: (layer, 0, j)),
                  pl.BlockSpec((sq, d, tn), lambda i, j: (layer, 0, nj + j)),
                  pl.BlockSpec((sq, FFN_CONV, tn), lambda i, j: (layer, 0, j)),
                  pl.BlockSpec((sq, FFN_CONV, tn), lambda i, j: (layer, 0, nj + j)),
                  pl.BlockSpec((sq, 1, tn), lambda i, j: (layer, 0, j)),
                  pl.BlockSpec((sq, 1, tn), lambda i, j: (layer, 0, nj + j))],
        out_specs=pl.BlockSpec((tm, tn), lambda i, j: (i, j)),
        out_shape=jax.ShapeDtypeStruct((rows, d_ff), BF16),
        scratch_shapes=[pltpu.VMEM((nj, 2, HALO_F32, tn), F32)],
        compiler_params=_cp(("arbitrary", "arbitrary"), 56),
        name=name,
    )(xn, w_up, w_up, conv_w, conv_w, cb, cb)


def _cumsum_chunks(x):
    row = lax.broadcasted_iota(jnp.int32, x.shape, 0) % SCAN_T
    sh = 1
    while sh < SCAN_T:
        x = x + jnp.where(row >= sh, pltpu.roll(x, sh, 0), 0.0)
        sh *= 2
    return x


def _chunk_rows(rows, reps):
    return jnp.concatenate([jnp.broadcast_to(r, (reps, r.shape[1])) for r in rows], axis=0)


def _hgrn2_body(q_ref, f_ref, v_ref, g_ref, lb_ref, hg_ref, o_ref, st_ref, *, nsub):
    @pl.when(pl.program_id(1) == 0)
    def _():
        st_ref[...] = jnp.zeros_like(st_ref)

    t = SCAN_T
    nblk = t // SUB_T
    lb = lb_ref[...]
    ri = lax.broadcasted_iota(jnp.int32, (t, t), 0)
    ci = lax.broadcasted_iota(jnp.int32, (t, t), 1)
    tril = ci <= ri

    q = q_ref[...]
    fp = f_ref[...]
    vb = v_ref[...].astype(BF16)
    row = lax.broadcasted_iota(jnp.int32, q.shape, 0) % t
    lf = jnp.log(lb + (1.0 - lb) * jax.nn.sigmoid(fp))
    k = (1.0 - lb) * jax.nn.sigmoid(-fp)
    b = _cumsum_chunks(lf)
    ref = [[b[c * t + u * SUB_T + SUB_T // 2 - 1:c * t + u * SUB_T + SUB_T // 2, :] for u in range(nblk)]
           for c in range(nsub)]
    last = [b[c * t + t - 1:c * t + t, :] for c in range(nsub)]
    r_own = _chunk_rows([ref[c][u] for c in range(nsub) for u in range(nblk)], SUB_T)
    qt = (q * jnp.exp(b - r_own)).astype(BF16)
    kh = [jnp.where(row < (u + 1) * SUB_T,
                    k * jnp.exp(_chunk_rows([ref[c][u] for c in range(nsub)], t) - b), 0.0).astype(BF16)
          for u in range(nblk)]
    kd = (k * jnp.exp(_chunk_rows(last, t) - b)).astype(BF16)
    qe = (q * jnp.exp(b)).astype(BF16)

    intra, upd = [], []
    for c in range(nsub):
        lo = c * t
        rows = [lax.dot_general(qt[lo + u * SUB_T:lo + (u + 1) * SUB_T, :], kh[u][lo:lo + t, :], NT_DIMS,
                                preferred_element_type=F32) for u in range(nblk)]
        a = jnp.where(tril, jnp.concatenate(rows, axis=0), 0.0).astype(BF16)
        intra.append(jnp.dot(a, vb[lo:lo + t, :], preferred_element_type=F32))
        upd.append(lax.dot_general(vb[lo:lo + t, :], kd[lo:lo + t, :], TN_DIMS, preferred_element_type=F32))

    st = st_ref[...]
    outs = []
    for c in range(nsub):
        outs.append(intra[c] + lax.dot_general(qe[c * t:(c + 1) * t, :], st.astype(BF16), NT_DIMS,
                                               preferred_element_type=F32))
        st = st * jnp.exp(last[c]) + upd[c]
    st_ref[...] = st
    o = _rms(jnp.concatenate(outs, axis=0), hg_ref[...])
    o_ref[...] = (o * jax.nn.silu(g_ref[...])).astype(o_ref.dtype)


def _hgrn2(proj, lb, hg, *, heads, tb, name):
    rows = proj.shape[0]
    blk = lambda off: pl.BlockSpec((tb, A_DK), lambda h, c, off=off: (c, off + h))
    vec = pl.BlockSpec((pl.Squeezed(), 1, A_DK), lambda h, c: (h, 0, 0))
    return pl.pallas_call(
        functools.partial(_hgrn2_body, nsub=tb // SCAN_T),
        grid=(heads, rows // tb),
        in_specs=[blk(0), blk(heads), blk(2 * heads), blk(3 * heads), vec, vec],
        out_specs=pl.BlockSpec((tb, A_DV), lambda h, c: (c, h)),
        out_shape=jax.ShapeDtypeStruct((rows, heads * A_DV), BF16),
        scratch_shapes=[pltpu.VMEM((A_DV, A_DK), F32)],
        compiler_params=_cp(("parallel", "arbitrary")),
        name=name,
    )(proj, proj, proj, proj, lb.reshape(heads, 1, A_DK), hg.reshape(heads, 1, A_DV))


def _kprep_body(kr_ref, cs_ref, o_ref):
    t = kr_ref[...] * cs_ref[...]
    o_ref[...] = (t + pltpu.roll(t, B_ROPE, 1)).astype(o_ref.dtype)


def _kprep(krp, cs, *, tm, name):
    rows = krp.shape[0]
    spec = pl.BlockSpec((tm, LANES), lambda i: (i, 0))
    return pl.pallas_call(
        _kprep_body, grid=(rows // tm,), in_specs=[spec, spec], out_specs=spec,
        out_shape=jax.ShapeDtypeStruct((rows, LANES), BF16),
        compiler_params=_cp(("parallel",)), name=name,
    )(krp, cs)


def _attn_body(qn_ref, qp_ref, cs_ref, kn_ref, kp_ref, vt_ref, o_ref, kcat_ref, *, tq, tk, n_kv, scale, hpb):
    i = pl.program_id(1)

    hs = [slice(h * LANES, (h + 1) * LANES) for h in range(hpb)]

    @pl.when(i == 0)
    def _():
        for h in range(hpb):
            kcat_ref[h, :, 0:B_NOPE] = kn_ref[:, hs[h]]
            kcat_ref[h, :, B_NOPE:] = kp_ref[...]

    qs = i * tq
    cs = cs_ref[...]
    qcat = [jnp.concatenate([(qn_ref[:, hs[h]].astype(F32) * scale).astype(BF16),
                             (qp_ref[:, hs[h]].astype(F32) * cs * scale).astype(BF16)], axis=1)
            for h in range(hpb)]
    off = CHUNK - N_META
    first_cid = (qs + off) // CHUNK
    last_cid = (qs + tq - 1 + off) // CHUNK
    n_full = jnp.minimum((CHUNK * first_cid + N_META) // tk, n_kv)
    n_keys = jnp.minimum(CHUNK * last_cid + N_META, n_kv * tk)
    n_wide = n_keys // tk
    ts = 2 * LANES
    n_small = (n_keys - n_wide * tk + ts - 1) // ts

    def make_step(masked, w):
        q_cid = (qs + lax.broadcasted_iota(jnp.int32, (w, tq), 1) + off) // CHUNK
        k_iota = lax.broadcasted_iota(jnp.int32, (w, tq), 0) + off

        def step(ks, carry):
            s = [lax.dot_general(kcat_ref[h, pl.ds(ks, w), :], qcat[h], NT_DIMS, preferred_element_type=F32)
                 for h in range(hpb)]
            out = []
            for h in range(hpb):
                m, l, acc = carry[h]
                sh = jnp.where((k_iota + ks) // CHUNK <= q_cid, s[h], -jnp.inf) if masked else s[h]
                m_new = jnp.maximum(m, jnp.max(sh, axis=0, keepdims=True))
                p = jnp.exp2(sh - m_new)
                alpha = jnp.exp2(m - m_new)
                l = alpha * l + jnp.sum(p, axis=0, keepdims=True)
                acc = alpha * acc + jnp.dot(vt_ref[hs[h], pl.ds(ks, w)], p.astype(BF16),
                                            preferred_element_type=F32)
                out.append((m_new, l, acc))
            return tuple(out)
        return step

    init = tuple((jnp.full((1, tq), -jnp.inf, F32), jnp.zeros((1, tq), F32), jnp.zeros((B_DV, tq), F32))
                 for _ in range(hpb))
    tile = lambda j: pl.multiple_of(j * tk, tk)
    plain, edge, small = make_step(False, tk), make_step(True, tk), make_step(True, ts)
    pairs = n_full // 2
    carry = lax.fori_loop(0, pairs, lambda jj, c: plain(tile(2 * jj + 1), plain(tile(2 * jj), c)), init)
    carry = lax.fori_loop(2 * pairs, n_full, lambda j, c: plain(tile(j), c), carry)
    carry = lax.fori_loop(n_full, n_wide, lambda j, c: edge(tile(j), c), carry)
    carry = lax.fori_loop(0, n_small, lambda t, c: small(pl.multiple_of(n_wide * tk + t * ts, LANES), c), carry)
    for h in range(hpb):
        _, l, acc = carry[h]
        o_ref[:, hs[h]] = (acc / l).T.astype(o_ref.dtype)


def _attention(qf, cs, kn, kp, vt, *, heads, tq, tk, name, hpb=2):
    rows = qf.shape[0]
    scale = float(B_NOPE + B_ROPE) ** -0.5 * math.log2(math.e)
    wide = hpb * LANES
    groups = heads // hpb
    return pl.pallas_call(
        functools.partial(_attn_body, tq=tq, tk=tk, n_kv=rows // tk, scale=scale, hpb=hpb),
        grid=(groups, rows // tq),
        in_specs=[pl.BlockSpec((tq, wide), lambda h, i: (i, h)),
                  pl.BlockSpec((tq, wide), lambda h, i: (i, groups + h)),
                  pl.BlockSpec((tq, LANES), lambda h, i: (i, 0)),
                  pl.BlockSpec((rows, wide), lambda h, i: (0, h)),
                  pl.BlockSpec((rows, LANES), lambda h, i: (0, 0)),
                  pl.BlockSpec((wide, rows), lambda h, i: (h, 0))],
        out_specs=pl.BlockSpec((tq, wide), lambda h, i: (i, h)),
        out_shape=jax.ShapeDtypeStruct((rows, heads * B_DV), BF16),
        scratch_shapes=[pltpu.VMEM((hpb, rows, B_NOPE + LANES), BF16)],
        compiler_params=_cp(("parallel", "arbitrary")),
        name=name,
    )(qf, qf, cs, kn, kp, vt)


def _pool_body(u_ref, halo_ref, w_ref, sc_ref, o_ref, *, tm, gdim):
    i = pl.program_id(0)
    hl = halo_ref[...]
    hl = jnp.where(i == 0, jnp.zeros_like(hl), hl)
    halo = hl.shape[0]
    pos = i * tm + lax.broadcasted_iota(jnp.int32, (tm, gdim), 0)
    for gi, win in enumerate(POOL_WINDOWS):
        cols = slice(gi * gdim, (gi + 1) * gdim)
        x = u_ref[:, cols]
        s = jnp.concatenate([hl[:, cols], x], axis=0)
        sh = 1
        while sh < win:
            s = s + pltpu.roll(s, sh, 0)
            sh *= 2
        cnt = jnp.minimum(pos + 1, win).astype(F32)
        pooled = s[halo:, :] / cnt - x
        y = jnp.dot(pooled.astype(BF16), w_ref[gi], preferred_element_type=F32)
        o_ref[:, cols] = (y * sc_ref[:, cols]).astype(o_ref.dtype)


def _pool(u, pool_w, pool_scale, *, tm, name):
    rows = u.shape[0]
    groups, gdim, _ = pool_w.shape
    cw = groups * gdim
    halo = 2 * HALO_F32
    hb = tm // halo
    return pl.pallas_call(
        functools.partial(_pool_body, tm=tm, gdim=gdim),
        grid=(rows // tm,),
        in_specs=[pl.BlockSpec((tm, cw), lambda i: (i, 0)),
                  pl.BlockSpec((halo, cw), lambda i: (jnp.maximum(i * hb - 1, 0), 0)),
                  pl.BlockSpec((groups, gdim, gdim), lambda i: (0, 0, 0)),
                  pl.BlockSpec((1, cw), lambda i: (0, 0))],
        out_specs=pl.BlockSpec((tm, cw), lambda i: (i, 0)),
        out_shape=jax.ShapeDtypeStruct((rows, cw), BF16),
        compiler_params=_cp(("parallel",)),
        name=name,
    )(u, u, pool_w.astype(BF16), pool_scale.reshape(1, cw))


def _split3_dot(x, sel):
    hi = x.astype(BF16)
    r1 = x - hi.astype(F32)
    mid = r1.astype(BF16)
    lo = (r1 - mid.astype(F32)).astype(BF16)
    return (jnp.dot(hi, sel, preferred_element_type=F32) + jnp.dot(mid, sel, preferred_element_type=F32)
            + jnp.dot(lo, sel, preferred_element_type=F32))


def _ssd_body(xs_ref, xh_ref, b_ref, bh_ref, c_ref, ch_ref, dt_ref, z_ref, cwx_ref, cwb_ref, cwc_ref,
              cbx_ref, cbb_ref, cbc_ref, dtb_ref, ah_ref, dsk_ref, o_ref, st_ref, *, nsub, hpg):
    g = pl.program_id(0)
    first = pl.program_id(1) == 0

    @pl.when(first)
    def _():
        st_ref[...] = jnp.zeros_like(st_ref)

    t = SCAN_T
    gw = hpg * D_HEAD_DIM

    def conv(x_ref, halo_ref, cw_ref, cb_ref):
        hl = halo_ref[...]
        x = jnp.concatenate([jnp.where(first, jnp.zeros_like(hl), hl), x_ref[...]], axis=0)
        cw = cw_ref[...]
        y = cw[D_CONV - 1:D_CONV, :] * x + cb_ref[...]
        for kk in range(1, D_CONV):
            y = y + cw[D_CONV - 1 - kk:D_CONV - kk, :] * pltpu.roll(x, kk, 0)
        return jax.nn.silu(y[HALO_F32:, :])

    xs = conv(xs_ref, xh_ref, cwx_ref, cbx_ref)
    bb = conv(b_ref, bh_ref, cwb_ref, cbb_ref).astype(BF16)
    cc = conv(c_ref, ch_ref, cwc_ref, cbc_ref).astype(BF16)
    dt = jax.nn.softplus(dt_ref[...] + dtb_ref[...])
    acs_h = _cumsum_chunks(dt * ah_ref[...])
    hrow = lax.broadcasted_iota(jnp.int32, (LANES, gw), 0)
    hcol = lax.broadcasted_iota(jnp.int32, (LANES, gw), 1) // D_HEAD_DIM + g * hpg
    sel = (hrow == hcol).astype(BF16)
    dtx = _split3_dot(dt, sel)
    acs = _split3_dot(acs_h, sel)
    xd = xs * dtx
    decay_in = jnp.exp(acs)
    last = [acs[c * t + t - 1:c * t + t, :] for c in range(nsub)]
    xdd = (xd * jnp.exp(_chunk_rows(last, t) - acs)).astype(BF16)

    pr = lax.broadcasted_iota(jnp.int32, (t, 2 * t), 0)
    pc = lax.broadcasted_iota(jnp.int32, (t, 2 * t), 1)
    pj = pc % t
    tril2 = pj <= pr
    eye2f = (pr == pj).astype(F32)
    left = pc < t

    diag, upd = [], []
    for c in range(nsub):
        rs = slice(c * t, (c + 1) * t)
        bc = bb[rs, :]
        cb2 = lax.dot_general(cc[rs, :], jnp.concatenate([bc, bc], axis=0), NT_DIMS, preferred_element_type=F32)
        parts = []
        for p in range(hpg // 2):
            cols = slice(p * 2 * D_HEAD_DIM, (p + 1) * 2 * D_HEAD_DIM)
            ac = acs[rs, cols]
            row_cs = jnp.sum(ac * eye2f, axis=0, keepdims=True)
            lmat = jnp.where(tril2, jnp.exp(ac - row_cs), 0.0)
            xdp = xd[rs, cols]
            rhs = jnp.concatenate([jnp.where(left, xdp, 0.0), jnp.where(left, 0.0, xdp)], axis=0)
            parts.append(jnp.dot((cb2 * lmat).astype(BF16), rhs.astype(BF16), preferred_element_type=F32))
        diag.append(jnp.concatenate(parts, axis=1))
        upd.append(lax.dot_general(bc, xdd[rs, :], TN_DIMS, preferred_element_type=F32))

    st = st_ref[...]
    ys = []
    for c in range(nsub):
        rs = slice(c * t, (c + 1) * t)
        ys.append(diag[c] + decay_in[rs, :] * jnp.dot(cc[rs, :], st.astype(BF16), preferred_element_type=F32))
        st = st * jnp.exp(last[c]) + upd[c]
    st_ref[...] = st
    y = jnp.concatenate(ys, axis=0) + xs * dsk_ref[...]
    o_ref[...] = y * jax.nn.silu(z_ref[...].astype(F32))


def _ssd(xbc, dt, z, conv_w, conv_b, dt_bias, a_h, d_ch, *, heads, tb, name):
    rows, ch = xbc.shape
    inner = heads * D_HEAD_DIM
    hpg = heads // D_GROUPS
    gw = hpg * D_HEAD_DIM
    assert hpg % 2 == 0 and 2 * D_HEAD_DIM == LANES and SCAN_T == D_HEAD_DIM
    nxb = inner // LANES
    hb = tb // HALO_F32
    prev = lambda c: jnp.maximum(c * hb - 1, 0)
    col_x = lambda g: g
    col_b = lambda g: nxb + g
    col_c = lambda g: nxb + D_GROUPS + g

    def data(width, col):
        return [pl.BlockSpec((tb, width), lambda g, c: (c, col(g))),
                pl.BlockSpec((HALO_F32, width), lambda g, c: (prev(c), col(g)))]

    def per_col(nrows, width, col):
        return pl.BlockSpec((nrows, width), lambda g, c: (0, col(g)))

    cb = conv_b.reshape(1, ch)
    return pl.pallas_call(
        functools.partial(_ssd_body, nsub=tb // SCAN_T, hpg=hpg),
        grid=(D_GROUPS, rows // tb),
        in_specs=(data(gw, col_x) + data(D_STATE, col_b) + data(D_STATE, col_c)
                  + [pl.BlockSpec((tb, LANES), lambda g, c: (c, 0)),
                     pl.BlockSpec((tb, gw), lambda g, c: (c, g)),
                     per_col(D_CONV, gw, col_x), per_col(D_CONV, D_STATE, col_b), per_col(D_CONV, D_STATE, col_c),
                     per_col(1, gw, col_x), per_col(1, D_STATE, col_b), per_col(1, D_STATE, col_c),
                     pl.BlockSpec((1, LANES), lambda g, c: (0, 0)),
                     pl.BlockSpec((1, LANES), lambda g, c: (0, 0)),
                     per_col(1, gw, col_x)]),
        out_specs=pl.BlockSpec((tb, gw), lambda g, c: (c, g)),
        out_shape=jax.ShapeDtypeStruct((rows, inner), F32),
        scratch_shapes=[pltpu.VMEM((D_STATE, gw), F32)],
        compiler_params=_cp(("parallel", "arbitrary")),
        name=name,
    )(xbc, xbc, xbc, xbc, xbc, xbc, dt, z, conv_w, conv_w, conv_w, cb, cb, cb, dt_bias, a_h, d_ch)


def _with_rot_half(w):
    half = B_ROPE // 2
    return jnp.concatenate([w, -w[..., half:], w[..., :half]], axis=-1)


def _layout_w_uq(w, heads):
    k = w.shape[0]
    w3 = w.astype(BF16).reshape(k, heads, B_NOPE + B_ROPE)
    nope = w3[:, :, :B_NOPE].reshape(k, heads * B_NOPE)
    pe = _with_rot_half(w3[:, :, B_NOPE:]).reshape(k, heads * LANES)
    return jnp.concatenate([nope, pe], axis=1)[None]


def kernel(x, meta_tokens, lb_logits, norm_g, ab_w_in, hgrn_norm_g, mla_q_norm_g, mla_kv_norm_g,
           mla_w_uq, mla_w_ukv, ab_w_out, cd_w_in, pool_w, pool_scale, ssm_conv_w, ssm_conv_b,
           ssm_dt_bias, ssm_a_log, ssm_d, ssm_norm_g, cd_w_out, ffn_w_up, ffn_conv_w, ffn_conv_b,
           ffn_w_down):
    bsz, seq, d = x.shape
    assert bsz == 1
    depth = norm_g.shape[0]
    total = N_META + seq
    lp = -(-total // Q_BLOCK) * Q_BLOCK
    a_heads = hgrn_norm_g.shape[1]
    a_kw = a_heads * A_DK
    a_w = a_heads * A_DV
    b_heads = mla_w_uq.shape[2] // (B_NOPE + B_ROPE)
    c_width = pool_scale.shape[1]
    d_inner = ssm_norm_g.shape[1]
    d_heads = ssm_dt_bias.shape[1]
    d_xbc = ssm_conv_w.shape[2]
    d_ff = ffn_w_down.shape[1]

    tm_mm = _pick(lp, (1040, 640, 512, 256, 128))
    tm_mid = _pick(lp, (640, 512, 256, 128))
    tm_row = _pick(lp, (320, 256, 128))
    tb = _pick(lp, (640, 320, 256, 128, 64))
    tq = _pick(lp, (640, 512, 256, 128))
    tn_ff = _pick(d_ff, (256, 128))
    tm_ff = _pick(lp, (1664, 1040, 640, 512, 256, 128))

    h = jnp.concatenate([meta_tokens.astype(x.dtype), x[0],
                         jnp.zeros((lp - total, d), x.dtype)], axis=0)

    inv = ROPE_THETA ** (-jnp.arange(0, B_ROPE, 2, dtype=F32) / B_ROPE)
    ang = jnp.arange(lp, dtype=F32)[:, None] * inv[None, :]
    cos, sin = jnp.cos(ang), jnp.sin(ang)
    cs = jnp.concatenate([cos, cos, sin, sin], axis=-1)
    lb_all = jnp.cumsum(jax.nn.softmax(lb_logits.astype(F32), axis=0), axis=0)
    w_up = ffn_w_up.astype(BF16)
    w_down = ffn_w_down.astype(BF16)

    xn = _rownorm(h, norm_g[0, 0], width=d, tm=tm_row, name="norm_in")

    for layer in range(depth):
        g = norm_g[layer]
        j = layer // 2
        xp = [(xn, d, 0, 0)]
        if layer % 2 == 0:
            n_ab = 2 * a_kw + 2 * a_w
            n_lora = B_Q_LORA + B_KV_LORA
            w_t = jnp.swapaxes(ab_w_in, 1, 2)
            proj_a = _matmul(xp, w_t, layer=j, col0=0, ncols=n_ab, tm=tm_mm, tn=512, out_dtype=F32, wt=True,
                             name="ab_in_a")
            proj_c = _matmul(xp, w_t, layer=j, col0=n_ab, ncols=n_lora, tm=tm_mm, tn=512, out_dtype=F32, wt=True,
                             name="ab_in_c")
            w_kr = _with_rot_half(w_t[j, n_ab + n_lora:, :].astype(BF16).T).T[None]
            krp = _matmul(xp, w_kr, ncols=LANES, tm=tm_mm, tn=LANES, out_dtype=F32, wt=True, name="ab_in_kr")
            o_a = _hgrn2(proj_a, lb_all[layer], hgrn_norm_g[j], heads=a_heads, tb=tb, name="hgrn2")
            cq = _rownorm(proj_c, mla_q_norm_g[j], width=B_Q_LORA, xblk=0, tm=tm_row, name="q_norm")
            ckv = _rownorm(proj_c, mla_kv_norm_g[j], width=B_KV_LORA, xblk=B_Q_LORA // B_KV_LORA, tm=tm_row,
                           name="kv_norm")
            w_uq = _layout_w_uq(mla_w_uq[j], b_heads)
            qf = _matmul([(cq, B_Q_LORA, 0, 0)], w_uq, ncols=w_uq.shape[2], tm=tm_mm, tn=512, out_dtype=BF16,
                         name="uq")
            w_kv = mla_w_ukv[j].astype(BF16).reshape(B_KV_LORA, b_heads, 2, B_NOPE)
            w_k = w_kv[:, :, 0, :].reshape(1, B_KV_LORA, b_heads * B_NOPE)
            w_vt = w_kv[:, :, 1, :].reshape(B_KV_LORA, b_heads * B_DV).T
            kn = _matmul([(ckv, B_KV_LORA, 0, 0)], w_k, ncols=b_heads * B_NOPE, tm=tm_mm, tn=512,
                         out_dtype=BF16, name="uk")
            vt = _matmul_nt(w_vt, ckv, tm=tm_mid, tn=512, name="uv_t")
            kp = _kprep(krp, cs, tm=tm_row, name="k_rope")
            o_b = _attention(qf, cs, kn, kp, vt, heads=b_heads, tq=tq, tk=tq, name="attention")
            half = a_w
            mix = [(o_a, half, 0, 0), (o_b, half, 0, 1)]
            w_out, w_out_layer = ab_w_out, j
        else:
            n_zu = d_inner + c_width
            w_t = jnp.swapaxes(cd_w_in, 1, 2)
            u_c = _matmul(xp, w_t, layer=j, col0=0, ncols=c_width, tm=tm_mm, tn=512, out_dtype=F32, wt=True,
                          name="cd_in_u")
            z = _matmul(xp, w_t, layer=j, col0=c_width, ncols=d_inner, tm=tm_mm, tn=512, out_dtype=BF16, wt=True,
                        name="cd_in_z")
            xbc = _matmul(xp, w_t, layer=j, col0=n_zu, ncols=d_xbc, tm=tm_mm, tn=512, out_dtype=F32, wt=True,
                          name="cd_in_xbc")
            dt_pad = LANES - d_heads
            w_dt = jnp.pad(w_t[j, n_zu + d_xbc:, :].astype(BF16), ((0, dt_pad), (0, 0)))[None]
            dt = _matmul(xp, w_dt, ncols=LANES, tm=tm_mm, tn=LANES, out_dtype=F32, wt=True, name="cd_in_dt")
            o_c = _pool(u_c, pool_w[j], pool_scale[j], tm=tm_row, name="pool")
            a_h = jnp.pad(-jnp.exp(ssm_a_log[j].astype(F32)), (0, dt_pad)).reshape(1, LANES)
            d_ch = jnp.repeat(ssm_d[j].astype(F32), D_HEAD_DIM).reshape(1, d_inner)
            dtb = jnp.pad(ssm_dt_bias[j].astype(F32), (0, dt_pad)).reshape(1, LANES)
            y = _ssd(xbc, dt, z, ssm_conv_w[j], ssm_conv_b[j], dtb, a_h, d_ch, heads=d_heads, tb=tb, name="ssd")
            yn = _rownorm(y, ssm_norm_g[j], width=d_inner, tm=tm_row, name="ssd_norm")
            mix = [(o_c, c_width, 0, 0)] + [(yn, c_width, b, b + 1) for b in range(d_inner // c_width)]
            w_out, w_out_layer = cd_w_out, j

        mixed = _matmul(mix, w_out, layer=w_out_layer, ncols=d, tm=tm_mm, tn=512, out_dtype=BF16, name="mix_out")
        h, xn = _addnorm(h, mixed, g[1], g[2], tm=tm_row, name="add_norm_mix")
        act = _ffn_up(xn, ffn_w_up, ffn_conv_w, ffn_conv_b, layer=layer, tm=tm_ff, tn=tn_ff, name="ffn_up")
        ff = _matmul([(act, d_ff, 0, 0)], w_down, layer=layer, ncols=d, tm=tm_mid, tn=256, out_dtype=BF16,
                     name="ffn_down")
        if layer + 1 < depth:
            h, xn = _addnorm(h, ff, g[3], norm_g[layer + 1, 0], tm=tm_row, name="add_norm_ffn")

    out = _add_rows(h, ff, norm_g[depth - 1, 3], row0=N_META, nrows=seq, tm=_pick(seq, (256, 240, 128, 16, 8)),
                    name="add_out")
    return out[None]
```
